```python
import jax, jax.numpy as jnp
from jax import lax
import numpy as np

D_MODEL = 1024
BATCH = 8
SEQ = 8192
DEPTH = 2
DEC_BATCH = 4
DEC_SEQ = 4096
PAST_LEN = 128

N_HEADS = 8
Q_LORA = 256
KV_LORA = 128
QK_NOPE = 64
QK_ROPE = 32
V_HEAD = 64
ROPE_BASE = 10000.0
ATTN_SCALE = (QK_NOPE + QK_ROPE) ** -0.5
Q_BLOCK = 128
CONV_CH = 512
CONV_W = 31
CONV_PAD = (CONV_W - 1) // 2
D_FF = 4 * D_MODEL
PLE_DIM = 256
EPS = 1e-6

OFF_KV = Q_LORA
OFF_KR = OFF_KV + KV_LORA
OFF_CONV = OFF_KR + QK_ROPE
OFF_GATE = OFF_CONV + 2 * CONV_CH
IN_COLS = OFF_GATE + 2 * D_MODEL

kernel_name = "mla_conformer_gated_hybrid_encoder"


def _rms(x, g):
    xf = x.astype(jnp.float32)
    y = xf * lax.rsqrt(jnp.mean(xf * xf, axis=-1, keepdims=True) + EPS)
    return (y * g.astype(jnp.float32)).astype(x.dtype)


def _rope_tables(seq, dtype):
    inv = 1.0 / (ROPE_BASE ** (jnp.arange(0, QK_ROPE, 2, dtype=jnp.float32) / QK_ROPE))
    ang = jnp.arange(seq, dtype=jnp.float32)[:, None] * inv[None, :]
    return jnp.cos(ang).astype(dtype), jnp.sin(ang).astype(dtype)


def _rope(x, cos, sin):
    half = QK_ROPE // 2
    x1, x2 = x[..., :half], x[..., half:]
    return jnp.concatenate([x1 * cos - x2 * sin, x1 * sin + x2 * cos], axis=-1)


def _mla_attention(q_nope, q_rope, k_nope, k_rope, v):
    b, s, h, _ = q_nope.shape
    nb = s // Q_BLOCK

    def block(args):
        qn, qr = args
        sc = (jnp.einsum('bqhn,bkhn->bhqk', qn, k_nope, preferred_element_type=jnp.float32)
              + jnp.einsum('bqhr,bkr->bhqk', qr, k_rope, preferred_element_type=jnp.float32)) * ATTN_SCALE
        pr = jax.nn.softmax(sc, axis=-1).astype(v.dtype)
        return jnp.einsum('bhqk,bkhv->bqhv', pr, v)

    qn_b = q_nope.reshape(b, nb, Q_BLOCK, h, QK_NOPE).transpose(1, 0, 2, 3, 4)
    qr_b = q_rope.reshape(b, nb, Q_BLOCK, h, QK_ROPE).transpose(1, 0, 2, 3, 4)
    out = lax.map(block, (qn_b, qr_b))
    return out.transpose(1, 0, 2, 3, 4).reshape(b, s, h * V_HEAD)


def _conv_branch(zc, conv_w, conv_b, ln_g, ln_b, w_pc):
    a, gt = zc[..., :CONV_CH], zc[..., CONV_CH:]
    glu = a * jax.nn.sigmoid(gt)
    y = lax.conv_general_dilated(glu, conv_w[:, None, :], window_strides=(1,),
                                 padding=[(CONV_PAD, CONV_PAD)],
                                 dimension_numbers=('NWC', 'WIO', 'NWC'),
                                 feature_group_count=CONV_CH) + conv_b
    yf = y.astype(jnp.float32)
    mu = jnp.mean(yf, axis=-1, keepdims=True)
    var = jnp.mean(jnp.square(yf - mu), axis=-1, keepdims=True)
    yn = ((yf - mu) * lax.rsqrt(var + EPS) * ln_g.astype(jnp.float32) + ln_b.astype(jnp.float32)).astype(zc.dtype)
    return jax.nn.silu(yn) @ w_pc


def _layer(x, p, cos, sin, g_mix, w_in, g_q, w_uq, g_kv, w_ukv, w_oa, conv_w, conv_b, ln_g, ln_b, w_pc,
           b_gate, w_out, g_mlp, w_up, w_down, w_ple_gate, w_ple, g_ple):
    b, s, _ = x.shape
    u = _rms(x, g_mix)
    z = u @ w_in
    cq = z[..., :OFF_KV]
    ckv = z[..., OFF_KV:OFF_KR]
    kr = z[..., OFF_KR:OFF_CONV]
    zc = z[..., OFF_CONV:OFF_GATE]
    zg = z[..., OFF_GATE:]
    q = (_rms(cq, g_q) @ w_uq).reshape(b, s, N_HEADS, QK_NOPE + QK_ROPE)
    q_nope = q[..., :QK_NOPE]
    q_rope = _rope(q[..., QK_NOPE:], cos[:, None, :], sin[:, None, :])
    k_rope = _rope(kr, cos, sin)
    kv = (_rms(ckv, g_kv) @ w_ukv).reshape(b, s, N_HEADS, QK_NOPE + V_HEAD)
    k_nope, v = kv[..., :QK_NOPE], kv[..., QK_NOPE:]
    attn = _mla_attention(q_nope, q_rope, k_nope, k_rope, v) @ w_oa
    conv = _conv_branch(zc, conv_w, conv_b, ln_g, ln_b, w_pc)
    gates = jax.nn.sigmoid(zg + b_gate)
    merged = gates[..., :D_MODEL] * attn + gates[..., D_MODEL:] * conv
    x = x + merged @ w_out
    h = _rms(x, g_mlp) @ w_up
    x = x + jnp.square(jax.nn.relu(h)) @ w_down
    x = x + jax.nn.sigmoid(x @ w_ple_gate) * _rms(p @ w_ple, g_ple)
    return x


def setup_inputs(seed: int = 0) -> dict:
    key = jax.random.key(seed)
    ks = jax.random.split(key, 32)

    def nrm(k, shape, scale):
        return jax.random.normal(k, shape, jnp.float32) * scale

    def gain(k, shape):
        return 1.0 + 0.01 * jax.random.normal(k, shape, jnp.float32)

    L = DEPTH
    return {
        "x_prompt": nrm(ks[0], (BATCH, SEQ, D_MODEL), 1.0),
        "x_sample": nrm(ks[1], (DEC_BATCH, DEC_SEQ, D_MODEL), 1.0),
        "p_prompt": nrm(ks[2], (DEPTH, BATCH, SEQ, PLE_DIM), 1.0),
        "p_sample": nrm(ks[3], (DEPTH, DEC_BATCH, DEC_SEQ, PLE_DIM), 1.0),
        "g_mix": gain(ks[4], (L, D_MODEL)),
        "w_in": nrm(ks[5], (L, D_MODEL, IN_COLS), D_MODEL ** -0.5),
        "g_q": gain(ks[6], (L, Q_LORA)),
        "w_uq": nrm(ks[7], (L, Q_LORA, N_HEADS * (QK_NOPE + QK_ROPE)), Q_LORA ** -0.5),
        "g_kv": gain(ks[8], (L, KV_LORA)),
        "w_ukv": nrm(ks[9], (L, KV_LORA, N_HEADS * (QK_NOPE + V_HEAD)), KV_LORA ** -0.5),
        "w_oa": nrm(ks[10], (L, N_HEADS * V_HEAD, D_MODEL), (N_HEADS * V_HEAD) ** -0.5),
        "conv_w": nrm(ks[11], (L, CONV_W, CONV_CH), CONV_W ** -0.5),
        "conv_b": nrm(ks[12], (L, CONV_CH), 0.01),
        "ln_g": gain(ks[13], (L, CONV_CH)),
        "ln_b": nrm(ks[14], (L, CONV_CH), 0.01),
        "w_pc": nrm(ks[15], (L, CONV_CH, D_MODEL), CONV_CH ** -0.5),
        "b_gate": nrm(ks[16], (L, 2 * D_MODEL), 0.01),
        "w_out": nrm(ks[17], (L, D_MODEL, D_MODEL), D_MODEL ** -0.5),
        "g_mlp": gain(ks[18], (L, D_MODEL)),
        "w_up": nrm(ks[19], (L, D_MODEL, D_FF), D_MODEL ** -0.5),
        "w_down": nrm(ks[20], (L, D_FF, D_MODEL), D_FF ** -0.5),
        "w_ple_gate": nrm(ks[21], (L, D_MODEL, D_MODEL), D_MODEL ** -0.5),
        "w_ple": nrm(ks[22], (L, PLE_DIM, D_MODEL), PLE_DIM ** -0.5),
        "g_ple": gain(ks[23], (L, D_MODEL)),
        "g_final": gain(ks[24], (D_MODEL,)),
    }


def reference(x_prompt, x_sample, p_prompt, p_sample, g_mix, w_in, g_q, w_uq, g_kv, w_ukv, w_oa,
              conv_w, conv_b, ln_g, ln_b, w_pc, b_gate, w_out, g_mlp, w_up, w_down,
              w_ple_gate, w_ple, g_ple, g_final):
    def run(x, p):
        cos, sin = _rope_tables(x.shape[1], x.dtype)
        for i in range(DEPTH):
            x = _layer(x, p[i], cos, sin, g_mix[i], w_in[i], g_q[i], w_uq[i], g_kv[i], w_ukv[i], w_oa[i],
                       conv_w[i], conv_b[i], ln_g[i], ln_b[i], w_pc[i], b_gate[i], w_out[i],
                       g_mlp[i], w_up[i], w_down[i], w_ple_gate[i], w_ple[i], g_ple[i])
        return _rms(x, g_final)

    y_prompt = run(x_prompt, p_prompt)
    y_sample = run(x_sample, p_sample)
    return (y_prompt, y_sample)
```

```python
import functools
import math

import jax
import jax.numpy as jnp
from jax import lax
from jax.experimental import pallas as pl
from jax.experimental.pallas import tpu as pltpu

D_MODEL = 1024
N_HEADS = 8
Q_LORA = 256
KV_LORA = 128
QK_NOPE = 64
QK_ROPE = 32
V_HEAD = 64
ROPE_BASE = 10000.0
ATTN_SCALE = (QK_NOPE + QK_ROPE) ** -0.5
CONV_CH = 512
CONV_W = 31
CONV_PAD = (CONV_W - 1) // 2
D_FF = 4 * D_MODEL
PLE_DIM = 256
EPS = 1e-6

OFF_KV = Q_LORA
OFF_KR = OFF_KV + KV_LORA
OFF_CONV = OFF_KR + QK_ROPE
OFF_GATE = OFF_CONV + 2 * CONV_CH

LANES = 128
HALO = 16
VMEM_LIMIT = 56 * 1024 * 1024
FF_CHUNK = 1024

F32 = jnp.float32
BF16 = jnp.bfloat16


def _rms(x, g):
    return x * lax.rsqrt(jnp.mean(x * x, axis=-1, keepdims=True) + EPS) * g


def _dot(a, b):
    return jnp.dot(a, b, preferred_element_type=F32)


def _dot_nt(a, b):
    return lax.dot_general(a, b, (((1,), (1,)), ((), ())), preferred_element_type=F32)


def _const_spec(shape):
    return pl.BlockSpec(shape, lambda *_: (0,) * len(shape), pipeline_mode=pl.Buffered(1))


def _pre_kernel(x_ref, gmix_ref, wlat_ref, wkr_ref, wconv_ref, gq_ref, wuq_ref, gkv_ref, wukt_ref,
                wuv_ref, cq_ref, sq_ref, ckt_ref, skt_ref, q_ref, kt_ref, v_ref, glu_ref):
    x = x_ref[...]
    u = _rms(x, gmix_ref[...]).astype(BF16)
    lat = _dot(u, wlat_ref[...])
    cqn = _rms(lat[:, :Q_LORA], gq_ref[...]).astype(BF16)
    ckvn = _rms(lat[:, Q_LORA:], gkv_ref[...]).astype(BF16)

    q2 = _dot(cqn, wuq_ref[...])
    cq_t, sq_t = cq_ref[...], sq_ref[...]
    hw = N_HEADS * LANES
    for h in range(N_HEADS):
        lo = h * LANES
        q_ref[:, lo:lo + LANES] = (q2[:, lo:lo + LANES] * cq_t
                                   + q2[:, hw + lo:hw + lo + LANES] * sq_t).astype(BF16)

    v_ref[...] = _dot(ckvn, wuv_ref[...]).astype(BF16)

    knt = _dot_nt(wukt_ref[...], ckvn)
    kr2 = _dot_nt(wkr_ref[...], u)
    krt = (kr2[:QK_ROPE] * ckt_ref[...] + kr2[QK_ROPE:] * skt_ref[...]).astype(BF16)
    zeros = jnp.zeros((LANES - QK_NOPE - QK_ROPE, krt.shape[1]), BF16)
    for h in range(N_HEADS):
        kt_ref[0, h, 0:QK_NOPE, :] = knt[h * QK_NOPE:(h + 1) * QK_NOPE].astype(BF16)
        kt_ref[0, h, QK_NOPE:QK_NOPE + QK_ROPE, :] = krt
        kt_ref[0, h, QK_NOPE + QK_ROPE:, :] = zeros

    zc = _dot(u, wconv_ref[...])
    glu_ref[...] = zc[:, :CONV_CH] * jax.nn.sigmoid(zc[:, CONV_CH:])


def _pre_call(x2d, lw, tabs, batch, seq, tm):
    t = batch * seq
    tiles_per_seq = seq // tm
    row = lambda i: (i, 0)
    in_specs = [
        pl.BlockSpec((tm, D_MODEL), row),
        _const_spec((1, D_MODEL)),
        _const_spec((D_MODEL, Q_LORA + KV_LORA)),
        _const_spec((2 * QK_ROPE, D_MODEL)),
        _const_spec((D_MODEL, 2 * CONV_CH)),
        _const_spec((1, Q_LORA)),
        _const_spec((Q_LORA, 2 * N_HEADS * LANES)),
        _const_spec((1, KV_LORA)),
        _const_spec((N_HEADS * QK_NOPE, KV_LORA)),
        _const_spec((KV_LORA, N_HEADS * V_HEAD)),
        pl.BlockSpec((tm, LANES), lambda i: (i % tiles_per_seq, 0)),
        pl.BlockSpec((tm, LANES), lambda i: (i % tiles_per_seq, 0)),
        pl.BlockSpec((QK_ROPE, tm), lambda i: (0, i % tiles_per_seq)),
        pl.BlockSpec((QK_ROPE, tm), lambda i: (0, i % tiles_per_seq)),
    ]
    out_specs = [
        pl.BlockSpec((tm, N_HEADS * LANES), row),
        pl.BlockSpec((1, N_HEADS, LANES, tm), lambda i: (i // tiles_per_seq, 0, 0, i % tiles_per_seq)),
        pl.BlockSpec((tm, N_HEADS * V_HEAD), row),
        pl.BlockSpec((tm, CONV_CH), row),
    ]
    out_shape = [
        jax.ShapeDtypeStruct((t, N_HEADS * LANES), BF16),
        jax.ShapeDtypeStruct((batch, N_HEADS, LANES, seq), BF16),
        jax.ShapeDtypeStruct((t, N_HEADS * V_HEAD), BF16),
        jax.ShapeDtypeStruct((t, CONV_CH), F32),
    ]
    return pl.pallas_call(
        _pre_kernel, grid=(t // tm,), in_specs=in_specs, out_specs=out_specs, out_shape=out_shape,
        compiler_params=pltpu.CompilerParams(dimension_semantics=("parallel",), vmem_limit_bytes=VMEM_LIMIT),
        name="pre",
    )(x2d, lw["g_mix"], lw["w_lat"], lw["w_kr"], lw["w_conv"], lw["g_q"], lw["w_uq"], lw["g_kv"],
      lw["w_ukt"], lw["w_uv"], tabs["cq"], tabs["sq"], tabs["ckt"], tabs["skt"])


def _attn_kernel(q_ref, kt_ref, v_ref, o_ref, m_sc, l_sc, acc_sc, *, tk, n_chunks):
    tq = q_ref.shape[0]
    outs = []
    for hh in range(2):
        q = q_ref[:, hh * LANES:(hh + 1) * LANES]
        m_sc[...] = jnp.full(m_sc.shape, -jnp.inf, F32)
        l_sc[...] = jnp.zeros(l_sc.shape, F32)
        acc_sc[...] = jnp.zeros(acc_sc.shape, F32)

        def body(j, carry, q=q, hh=hh):
            start = pl.multiple_of(j * tk, tk)
            kt = kt_ref[0, hh, :, pl.ds(start, tk)]
            s = _dot(q, kt)
            m_prev = m_sc[...]
            m_next = jnp.maximum(m_prev, jnp.max(s, axis=1, keepdims=True))
            p = jnp.exp2(s - jnp.tile(m_next, (1, tk // LANES)))
            alpha = jnp.exp2(m_prev - m_next)
            l_sc[...] = alpha * l_sc[...] + jnp.sum(p, axis=1, keepdims=True)
            m_sc[...] = m_next
            pv = _dot(p.astype(BF16), v_ref[pl.ds(start, tk), :])
            acc_sc[...] = alpha * acc_sc[...] + pv
            return carry

        lax.fori_loop(0, n_chunks, body, 0)
        outs.append(acc_sc[...] / l_sc[...])
    lane = lax.broadcasted_iota(jnp.int32, (tq, LANES), 1)
    o_ref[...] = jnp.where(lane < V_HEAD, outs[0], outs[1]).astype(o_ref.dtype)


def _attn_call(q, kt, v, batch, seq, tq, tk):
    t = batch * seq
    nq = seq // tq
    pairs = N_HEADS // 2
    kernel = functools.partial(_attn_kernel, tk=tk, n_chunks=seq // tk)
    return pl.pallas_call(
        kernel,
        grid=(batch, pairs, nq),
        in_specs=[
            pl.BlockSpec((tq, 2 * LANES), lambda b, g, i: (b * nq + i, g)),
            pl.BlockSpec((1, 2, LANES, seq), lambda b, g, i: (b, g, 0, 0)),
            pl.BlockSpec((seq, 2 * V_HEAD), lambda b, g, i: (b, g)),
        ],
        out_specs=pl.BlockSpec((tq, 2 * V_HEAD), lambda b, g, i: (b * nq + i, g)),
        out_shape=jax.ShapeDtypeStruct((t, N_HEADS * V_HEAD), BF16),
        scratch_shapes=[pltpu.VMEM((tq, LANES), F32), pltpu.VMEM((tq, LANES), F32),
                        pltpu.VMEM((tq, 2 * V_HEAD), F32)],
        compiler_params=pltpu.CompilerParams(dimension_semantics=("parallel", "parallel", "arbitrary"),
                                             vmem_limit_bytes=VMEM_LIMIT),
        name="attn",
    )(q, kt, v)


def _merge_kernel(x_ref, attn_ref, gprev_ref, gcur_ref, gnext_ref, gmix_ref, wgate_ref, bgate_ref, woa_ref,
                  convw_ref, convb_ref, lng_ref, lnb_ref, wpc_ref, wout_ref, o_ref, gext_sc, *, tiles_per_seq):
    tm = x_ref.shape[0]
    i = pl.program_id(0)
    x = x_ref[...]
    u = _rms(x, gmix_ref[...]).astype(BF16)
    gates = jax.nn.sigmoid(_dot(u, wgate_ref[...]) + bgate_ref[...])
    attn = _dot(attn_ref[...], woa_ref[...])

    pos = i % tiles_per_seq
    gext_sc[0:HALO, :] = jnp.where(pos == 0, 0.0, gprev_ref[...])
    gext_sc[HALO:HALO + tm, :] = gcur_ref[...]
    gext_sc[HALO + tm:, :] = jnp.where(pos == tiles_per_seq - 1, 0.0, gnext_ref[...])
    y = jnp.broadcast_to(convb_ref[...], (tm, CONV_CH))
    for k in range(CONV_W):
        off = HALO - CONV_PAD + k
        y = y + convw_ref[k:k + 1, :] * gext_sc[off:off + tm, :]
    mu = jnp.mean(y, axis=-1, keepdims=True)
    yc = y - mu
    var = jnp.mean(yc * yc, axis=-1, keepdims=True)
    yn = yc * lax.rsqrt(var + EPS) * lng_ref[...] + lnb_ref[...]
    conv = _dot((yn * jax.nn.sigmoid(yn)).astype(BF16), wpc_ref[...])

    merged = gates[:, :D_MODEL] * attn + gates[:, D_MODEL:] * conv
    o_ref[...] = x + _dot(merged.astype(BF16), wout_ref[...])


def _merge_call(x2d, attn, glu, lw, batch, seq, tm):
    t = batch * seq
    tiles_per_seq = seq // tm
    hb = tm // HALO
    n_hblk = t // HALO
    row = lambda i: (i, 0)
    kernel = functools.partial(_merge_kernel, tiles_per_seq=tiles_per_seq)
    in_specs = [
        pl.BlockSpec((tm, D_MODEL), row),
        pl.BlockSpec((tm, N_HEADS * V_HEAD), row),
        pl.BlockSpec((HALO, CONV_CH), lambda i: (jnp.maximum(i * hb - 1, 0), 0)),
        pl.BlockSpec((tm, CONV_CH), row),
        pl.BlockSpec((HALO, CONV_CH), lambda i: (jnp.minimum((i + 1) * hb, n_hblk - 1), 0)),
        _const_spec((1, D_MODEL)),
        _const_spec((D_MODEL, 2 * D_MODEL)),
        _const_spec((1, 2 * D_MODEL)),
        _const_spec((N_HEADS * V_HEAD, D_MODEL)),
        _const_spec((CONV_W, CONV_CH)),
        _const_spec((1, CONV_CH)),
        _const_spec((1, CONV_CH)),
        _const_spec((1, CONV_CH)),
        _const_spec((CONV_CH, D_MODEL)),
        _const_spec((D_MODEL, D_MODEL)),
    ]
    return pl.pallas_call(
        kernel, grid=(t // tm,), in_specs=in_specs,
        out_specs=pl.BlockSpec((tm, D_MODEL), row),
        out_shape=jax.ShapeDtypeStruct((t, D_MODEL), F32),
        scratch_shapes=[pltpu.VMEM((tm + 2 * HALO, CONV_CH), F32)],
        compiler_params=pltpu.CompilerParams(dimension_semantics=("parallel",), vmem_limit_bytes=VMEM_LIMIT),
        name="merge",
    )(x2d, attn, glu, glu, glu, lw["g_mix"], lw["w_gate"], lw["b_gate"], lw["w_oa"], lw["conv_w"],
      lw["conv_b"], lw["ln_g"], lw["ln_b"], lw["w_pc"], lw["w_out"])


def _mlp_kernel(x_ref, p_ref, gmlp_ref, wup_ref, wdown_ref, wpg_ref, wple_ref, gple_ref, gfin_ref, o_ref, *,
                final):
    x = x_ref[...]
    un = _rms(x, gmlp_ref[...]).astype(BF16)
    acc = x
    for c in range(D_FF // FF_CHUNK):
        lo = c * FF_CHUNK
        h = jnp.maximum(_dot(un, wup_ref[:, lo:lo + FF_CHUNK]), 0.0)
        acc = acc + _dot((h * h).astype(BF16), wdown_ref[lo:lo + FF_CHUNK, :])
    gate = jax.nn.sigmoid(_dot(acc.astype(BF16), wpg_ref[...]))
    emb = _rms(_dot(p_ref[...].astype(BF16), wple_ref[...]), gple_ref[...])
    out = acc + gate * emb
    if final:
        out = _rms(out, gfin_ref[...])
    o_ref[...] = out


def _mlp_call(x2d, p2d, lw, g_final, tm, final):
    t = x2d.shape[0]
    row = lambda i: (i, 0)
    kernel = functools.partial(_mlp_kernel, final=final)
    in_specs = [
        pl.BlockSpec((tm, D_MODEL), row),
        pl.BlockSpec((tm, PLE_DIM), row),
        _const_spec((1, D_MODEL)),
        _const_spec((D_MODEL, D_FF)),
        _const_spec((D_FF, D_MODEL)),
        _const_spec((D_MODEL, D_MODEL)),
        _const_spec((PLE_DIM, D_MODEL)),
        _const_spec((1, D_MODEL)),
        _const_spec((1, D_MODEL)),
    ]
    return pl.pallas_call(
        kernel, grid=(t // tm,), in_specs=in_specs,
        out_specs=pl.BlockSpec((tm, D_MODEL), row),
        out_shape=jax.ShapeDtypeStruct((t, D_MODEL), F32),
        compiler_params=pltpu.CompilerParams(dimension_semantics=("parallel",), vmem_limit_bytes=VMEM_LIMIT),
        name="mlp",
    )(x2d, p2d, lw["g_mlp"], lw["w_up"], lw["w_down"], lw["w_pg"], lw["w_ple"], lw["g_ple"], g_final)


def _rot_half_cols(w):
    half = QK_ROPE // 2
    return jnp.concatenate([-w[..., half:], w[..., :half]], axis=-1)


def _layer_weights(i, g_mix, w_in, g_q, w_uq, g_kv, w_ukv, w_oa, conv_w, conv_b, ln_g, ln_b, w_pc, b_gate,
                   w_out, g_mlp, w_up, w_down, w_ple_gate, w_ple, g_ple):
    wi = w_in[i]
    w_kr = wi[:, OFF_KR:OFF_CONV]
    w_kr2 = jnp.concatenate([w_kr, _rot_half_cols(w_kr)], axis=1).T

    wq = w_uq[i].reshape(Q_LORA, N_HEADS, QK_NOPE + QK_ROPE)
    pad = jnp.zeros((Q_LORA, N_HEADS, LANES - QK_NOPE - QK_ROPE), F32)
    wq_plain = jnp.concatenate([wq, pad], axis=-1).reshape(Q_LORA, N_HEADS * LANES)
    wq_rot = jnp.concatenate([jnp.zeros((Q_LORA, N_HEADS, QK_NOPE), F32), _rot_half_cols(wq[..., QK_NOPE:]), pad],
                             axis=-1).reshape(Q_LORA, N_HEADS * LANES)

    wkv = w_ukv[i].reshape(KV_LORA, N_HEADS, QK_NOPE + V_HEAD)
    w_uk = wkv[..., :QK_NOPE].reshape(KV_LORA, N_HEADS * QK_NOPE)
    w_uv = wkv[..., QK_NOPE:].reshape(KV_LORA, N_HEADS * V_HEAD)
    r1 = lambda a: a[i].reshape(1, -1)
    return {
        "g_mix": r1(g_mix), "g_q": r1(g_q), "g_kv": r1(g_kv), "g_mlp": r1(g_mlp), "g_ple": r1(g_ple),
        "b_gate": r1(b_gate), "conv_b": r1(conv_b), "ln_g": r1(ln_g), "ln_b": r1(ln_b),
        "conv_w": conv_w[i],
        "w_lat": wi[:, :OFF_KR].astype(BF16),
        "w_kr": w_kr2.astype(BF16),
        "w_conv": wi[:, OFF_CONV:OFF_GATE].astype(BF16),
        "w_gate": wi[:, OFF_GATE:].astype(BF16),
        "w_uq": jnp.concatenate([wq_plain, wq_rot], axis=1).astype(BF16),
        "w_ukt": w_uk.T.astype(BF16),
        "w_uv": w_uv.astype(BF16),
        "w_oa": w_oa[i].astype(BF16),
        "w_pc": w_pc[i].astype(BF16),
        "w_out": w_out[i].astype(BF16),
        "w_up": w_up[i].astype(BF16),
        "w_down": w_down[i].astype(BF16),
        "w_pg": w_ple_gate[i].astype(BF16),
        "w_ple": w_ple[i].astype(BF16),
    }


def _rope_tables(seq):
    inv = 1.0 / (ROPE_BASE ** (jnp.arange(0, QK_ROPE, 2, dtype=F32) / QK_ROPE))
    ang = jnp.arange(seq, dtype=F32)[:, None] * inv[None, :]
    cos, sin = jnp.cos(ang), jnp.sin(ang)
    scale = ATTN_SCALE * math.log2(math.e)
    ones = jnp.ones((seq, QK_NOPE), F32)
    zpad = jnp.zeros((seq, LANES - QK_NOPE - QK_ROPE), F32)
    cq = jnp.concatenate([ones, cos, cos, zpad], axis=1) * scale
    sq = jnp.concatenate([jnp.zeros((seq, QK_NOPE), F32), sin, sin, zpad], axis=1) * scale
    ckt = jnp.concatenate([cos, cos], axis=1).T
    skt = jnp.concatenate([sin, sin], axis=1).T
    return {"cq": cq, "sq": sq, "ckt": ckt, "skt": skt}


def _pick_tile(n, target):
    tile = min(n, target)
    assert n % tile == 0, (n, tile)
    return tile


def _run_group(x, p, layers, g_final):
    batch, seq, _ = x.shape
    depth = p.shape[0]
    tm = _pick_tile(seq, 512)
    tq = _pick_tile(seq, 512)
    tk = _pick_tile(seq, 512)
    tabs = _rope_tables(seq)
    x2d = x.reshape(batch * seq, D_MODEL)
    p3d = p.reshape(depth, batch * seq, PLE_DIM)
    gfin = g_final.reshape(1, D_MODEL)
    for i in range(depth):
        lw = layers[i]
        q, kt, v, glu = _pre_call(x2d, lw, tabs, batch, seq, tm)
        attn = _attn_call(q, kt, v, batch, seq, tq, tk)
        x2d = _merge_call(x2d, attn, glu, lw, batch, seq, tm)
        x2d = _mlp_call(x2d, p3d[i], lw, gfin, tm, final=(i == depth - 1))
    return x2d.reshape(batch, seq, D_MODEL)


def kernel(x_prompt, x_sample, p_prompt, p_sample, g_mix, w_in, g_q, w_uq, g_kv, w_ukv, w_oa, conv_w, conv_b,
           ln_g, ln_b, w_pc, b_gate, w_out, g_mlp, w_up, w_down, w_ple_gate, w_ple, g_ple, g_final):
    depth = w_in.shape[0]
    layers = [_layer_weights(i, g_mix, w_in, g_q, w_uq, g_kv, w_ukv, w_oa, conv_w, conv_b, ln_g, ln_b, w_pc,
                             b_gate, w_out, g_mlp, w_up, w_down, w_ple_gate, w_ple, g_ple) for i in range(depth)]
    y_prompt = _run_group(x_prompt, p_prompt, layers, g_final)
    y_sample = _run_group(x_sample, p_sample, layers, g_final)
    return (y_prompt, y_sample)
```

```python
import functools
import math

import jax
import jax.numpy as jnp
from jax import lax
from jax.experimental import pallas as pl
from jax.experimental.pallas import tpu as pltpu

D_MODEL = 1024
N_HEADS = 8
Q_LORA = 256
KV_LORA = 128
QK_NOPE = 64
QK_ROPE = 32
V_HEAD = 64
ROPE_BASE = 10000.0
ATTN_SCALE = (QK_NOPE + QK_ROPE) ** -0.5
CONV_CH = 512
CONV_W = 31
CONV_PAD = (CONV_W - 1) // 2
D_FF = 4 * D_MODEL
PLE_DIM = 256
EPS = 1e-6

OFF_KV = Q_LORA
OFF_KR = OFF_KV + KV_LORA
OFF_CONV = OFF_KR + QK_ROPE
OFF_GATE = OFF_CONV + 2 * CONV_CH

LANES = 128
SUBLANES = 8
HALO = 16
VMEM_LIMIT = 56 * 1024 * 1024
FF_CHUNK = 1024

F32 = jnp.float32
BF16 = jnp.bfloat16


def _rms(x, g):
    return x * lax.rsqrt(jnp.mean(x * x, axis=-1, keepdims=True) + EPS) * g


def _dot(a, b):
    return jnp.dot(a, b, preferred_element_type=F32)


def _dot_nt(a, b):
    return lax.dot_general(a, b, (((1,), (1,)), ((), ())), preferred_element_type=F32)


def _const_spec(shape):
    return pl.BlockSpec(shape, lambda *_: (0,) * len(shape), pipeline_mode=pl.Buffered(1))


def _pre_kernel(x_ref, gmix_ref, wlat_ref, wkr_ref, wconv_ref, gq_ref, wuq_ref, gkv_ref, wukt_ref,
                wuv_ref, cq_ref, sq_ref, ckt_ref, skt_ref, q_ref, kt_ref, v_ref, glu_ref):
    x = x_ref[...]
    u = _rms(x, gmix_ref[...]).astype(BF16)
    lat = _dot(u, wlat_ref[...])
    cqn = _rms(lat[:, :Q_LORA], gq_ref[...]).astype(BF16)
    ckvn = _rms(lat[:, Q_LORA:], gkv_ref[...]).astype(BF16)

    q2 = _dot(cqn, wuq_ref[...])
    cq_t, sq_t = cq_ref[...], sq_ref[...]
    hw = N_HEADS * LANES
    for h in range(N_HEADS):
        lo = h * LANES
        q_ref[:, lo:lo + LANES] = (q2[:, lo:lo + LANES] * cq_t
                                   + q2[:, hw + lo:hw + lo + LANES] * sq_t).astype(BF16)

    v = _dot(ckvn, wuv_ref[...]).astype(BF16)
    ones = jnp.ones((v.shape[0], LANES), BF16)
    for g in range(N_HEADS // 2):
        v_ref[:, 2 * g * LANES:(2 * g + 1) * LANES] = v[:, g * LANES:(g + 1) * LANES]
        v_ref[:, (2 * g + 1) * LANES:(2 * g + 2) * LANES] = ones

    knt = _dot_nt(wukt_ref[...], ckvn)
    kr2 = _dot_nt(wkr_ref[...], u)
    krt = (kr2[:QK_ROPE] * ckt_ref[...] + kr2[QK_ROPE:] * skt_ref[...]).astype(BF16)
    zeros = jnp.zeros((LANES - QK_NOPE - QK_ROPE, krt.shape[1]), BF16)
    for h in range(N_HEADS):
        kt_ref[0, h, 0:QK_NOPE, :] = knt[h * QK_NOPE:(h + 1) * QK_NOPE].astype(BF16)
        kt_ref[0, h, QK_NOPE:QK_NOPE + QK_ROPE, :] = krt
        kt_ref[0, h, QK_NOPE + QK_ROPE:, :] = zeros

    zc = _dot(u, wconv_ref[...])
    glu_ref[...] = zc[:, :CONV_CH] * jax.nn.sigmoid(zc[:, CONV_CH:])


def _pre_call(x2d, lw, tabs, batch, seq, tm):
    t = batch * seq
    tiles_per_seq = seq // tm
    row = lambda i: (i, 0)
    in_specs = [
        pl.BlockSpec((tm, D_MODEL), row),
        _const_spec((1, D_MODEL)),
        _const_spec((D_MODEL, Q_LORA + KV_LORA)),
        _const_spec((2 * QK_ROPE, D_MODEL)),
        _const_spec((D_MODEL, 2 * CONV_CH)),
        _const_spec((1, Q_LORA)),
        _const_spec((Q_LORA, 2 * N_HEADS * LANES)),
        _const_spec((1, KV_LORA)),
        _const_spec((N_HEADS * QK_NOPE, KV_LORA)),
        _const_spec((KV_LORA, N_HEADS * V_HEAD)),
        pl.BlockSpec((tm, LANES), lambda i: (i % tiles_per_seq, 0)),
        pl.BlockSpec((tm, LANES), lambda i: (i % tiles_per_seq, 0)),
        pl.BlockSpec((QK_ROPE, tm), lambda i: (0, i % tiles_per_seq)),
        pl.BlockSpec((QK_ROPE, tm), lambda i: (0, i % tiles_per_seq)),
    ]
    out_specs = [
        pl.BlockSpec((tm, N_HEADS * LANES), row),
        pl.BlockSpec((1, N_HEADS, LANES, tm), lambda i: (i // tiles_per_seq, 0, 0, i % tiles_per_seq)),
        pl.BlockSpec((tm, N_HEADS * LANES), row),
        pl.BlockSpec((tm, CONV_CH), row),
    ]
    out_shape = [
        jax.ShapeDtypeStruct((t, N_HEADS * LANES), BF16),
        jax.ShapeDtypeStruct((batch, N_HEADS, LANES, seq), BF16),
        jax.ShapeDtypeStruct((t, N_HEADS * LANES), BF16),
        jax.ShapeDtypeStruct((t, CONV_CH), F32),
    ]
    return pl.pallas_call(
        _pre_kernel, grid=(t // tm,), in_specs=in_specs, out_specs=out_specs, out_shape=out_shape,
        compiler_params=pltpu.CompilerParams(dimension_semantics=("parallel",), vmem_limit_bytes=VMEM_LIMIT),
        name="pre",
    )(x2d, lw["g_mix"], lw["w_lat"], lw["w_kr"], lw["w_conv"], lw["g_q"], lw["w_uq"], lw["g_kv"],
      lw["w_ukt"], lw["w_uv"], tabs["cq"], tabs["sq"], tabs["ckt"], tabs["skt"])


def _attn_kernel(q_ref, kt_ref, v_ref, o_ref, s_sc, m_sc, ml_sc, acc_sc, *, tk, n_chunks):
    tq = q_ref.shape[0]
    reps = tk // LANES

    def chunk_start(c):
        return c * tk

    outs = []
    for hh in range(2):
        q = q_ref[:, hh * LANES:(hh + 1) * LANES]

        def stage_a(c, slot, q=q, hh=hh):
            s = _dot(q, kt_ref[0, hh, :, pl.ds(chunk_start(c), tk)])
            s_sc[slot] = s
            ml_sc[slot] = jnp.broadcast_to(jnp.max(s, axis=1, keepdims=True), (tq, LANES))

        def stage_b(c, slot):
            m_prev = m_sc[...]
            m_next = jnp.maximum(m_prev, ml_sc[slot])
            p = jnp.exp2(s_sc[slot] - jnp.tile(m_next, (1, reps)))
            alpha = jnp.exp2(m_prev - m_next)
            pv = _dot(p.astype(BF16), v_ref[pl.ds(chunk_start(c), tk), :])
            acc_sc[...] = jnp.tile(alpha, (1, 2)) * acc_sc[...] + pv
            m_sc[...] = m_next

        m_sc[...] = jnp.full(m_sc.shape, -jnp.inf, F32)
        acc_sc[...] = jnp.zeros(acc_sc.shape, F32)
        stage_a(0, 0)
        for c in range(n_chunks):
            if c + 1 < n_chunks:
                stage_a(c + 1, (c + 1) % 2)
            stage_b(c, c % 2)
        acc = acc_sc[...]
        outs.append(acc[:, :LANES] / acc[:, LANES:])
    lane = lax.broadcasted_iota(jnp.int32, (tq, LANES), 1)
    o_ref[...] = jnp.where(lane < V_HEAD, outs[0], outs[1]).astype(o_ref.dtype)


def _attn_call(q, kt, v, batch, seq, tq, tk):
    t = batch * seq
    nq = seq // tq
    pairs = N_HEADS // 2
    n_chunks = seq // tk
    kernel = functools.partial(_attn_kernel, tk=tk, n_chunks=n_chunks)
    return pl.pallas_call(
        kernel,
        grid=(batch, pairs, nq),
        in_specs=[
            pl.BlockSpec((tq, 2 * LANES), lambda b, g, i: (b * nq + i, g)),
            pl.BlockSpec((1, 2, LANES, seq), lambda b, g, i: (b, g, 0, 0)),
            pl.BlockSpec((seq, 2 * LANES), lambda b, g, i: (b, g)),
        ],
        out_specs=pl.BlockSpec((tq, 2 * V_HEAD), lambda b, g, i: (b * nq + i, g)),
        out_shape=jax.ShapeDtypeStruct((t, N_HEADS * V_HEAD), BF16),
        scratch_shapes=[pltpu.VMEM((2, tq, tk), F32), pltpu.VMEM((tq, LANES), F32),
                        pltpu.VMEM((2, tq, LANES), F32), pltpu.VMEM((tq, 2 * LANES), F32)],
        compiler_params=pltpu.CompilerParams(dimension_semantics=("parallel", "parallel", "arbitrary"),
                                             vmem_limit_bytes=VMEM_LIMIT),
        name="attn",
    )(q, kt, v)


def _merge_kernel(x_ref, attn_ref, gprev_ref, gcur_ref, gnext_ref, gmix_ref, wgate_ref, bgate_ref, woa_ref,
                  convw_ref, convb_ref, lng_ref, lnb_ref, wpc_ref, wout_ref, o_ref, gext_sc, *, tiles_per_seq):
    tm = x_ref.shape[0]
    i = pl.program_id(0)
    x = x_ref[...]
    u = _rms(x, gmix_ref[...]).astype(BF16)
    gates = jax.nn.sigmoid(_dot(u, wgate_ref[...]) + bgate_ref[...])
    attn = _dot(attn_ref[...], woa_ref[...])

    pos = i % tiles_per_seq
    gext_sc[0:HALO, :] = jnp.where(pos == 0, 0.0, gprev_ref[...])
    gext_sc[HALO:HALO + tm, :] = gcur_ref[...]
    gext_sc[HALO + tm:, :] = jnp.where(pos == tiles_per_seq - 1, 0.0, gnext_ref[...])
    y = jnp.broadcast_to(convb_ref[...], (tm, CONV_CH))
    for r in range(SUBLANES):
        part = None
        for k in range(CONV_W):
            off = HALO - CONV_PAD + k
            if off % SUBLANES != r:
                continue
            base = off - r
            term = convw_ref[k:k + 1, :] * gext_sc[base:base + tm + SUBLANES, :]
            part = term if part is None else part + term
        if part is not None:
            y = y + part[r:r + tm, :]
    mu = jnp.mean(y, axis=-1, keepdims=True)
    yc = y - mu
    var = jnp.mean(yc * yc, axis=-1, keepdims=True)
    yn = yc * lax.rsqrt(var + EPS) * lng_ref[...] + lnb_ref[...]
    conv = _dot((yn * jax.nn.sigmoid(yn)).astype(BF16), wpc_ref[...])

    merged = gates[:, :D_MODEL] * attn + gates[:, D_MODEL:] * conv
    o_ref[...] = x + _dot(merged.astype(BF16), wout_ref[...])


def _merge_call(x2d, attn, glu, lw, batch, seq, tm):
    t = batch * seq
    tiles_per_seq = seq // tm
    hb = tm // HALO
    n_hblk = t // HALO
    row = lambda i: (i, 0)
    kernel = functools.partial(_merge_kernel, tiles_per_seq=tiles_per_seq)
    in_specs = [
        pl.BlockSpec((tm, D_MODEL), row),
        pl.BlockSpec((tm, N_HEADS * V_HEAD), row),
        pl.BlockSpec((HALO, CONV_CH), lambda i: (jnp.maximum(i * hb - 1, 0), 0)),
        pl.BlockSpec((tm, CONV_CH), row),
        pl.BlockSpec((HALO, CONV_CH), lambda i: (jnp.minimum((i + 1) * hb, n_hblk - 1), 0)),
        _const_spec((1, D_MODEL)),
        _const_spec((D_MODEL, 2 * D_MODEL)),
        _const_spec((1, 2 * D_MODEL)),
        _const_spec((N_HEADS * V_HEAD, D_MODEL)),
        _const_spec((CONV_W, CONV_CH)),
        _const_spec((1, CONV_CH)),
        _const_spec((1, CONV_CH)),
        _const_spec((1, CONV_CH)),
        _const_spec((CONV_CH, D_MODEL)),
        _const_spec((D_MODEL, D_MODEL)),
    ]
    return pl.pallas_call(
        kernel, grid=(t // tm,), in_specs=in_specs,
        out_specs=pl.BlockSpec((tm, D_MODEL), row),
        out_shape=jax.ShapeDtypeStruct((t, D_MODEL), F32),
        scratch_shapes=[pltpu.VMEM((tm + 2 * HALO, CONV_CH), F32)],
        compiler_params=pltpu.CompilerParams(dimension_semantics=("parallel",), vmem_limit_bytes=VMEM_LIMIT),
        name="merge",
    )(x2d, attn, glu, glu, glu, lw["g_mix"], lw["w_gate"], lw["b_gate"], lw["w_oa"], lw["conv_w"],
      lw["conv_b"], lw["ln_g"], lw["ln_b"], lw["w_pc"], lw["w_out"])


def _mlp_kernel(x_ref, p_ref, gmlp_ref, wup_ref, wdown_ref, wpg_ref, wple_ref, gple_ref, gfin_ref, o_ref, *,
                final):
    x = x_ref[...]
    un = _rms(x, gmlp_ref[...]).astype(BF16)
    acc = x
    for c in range(D_FF // FF_CHUNK):
        lo = c * FF_CHUNK
        h = jnp.maximum(_dot(un, wup_ref[:, lo:lo + FF_CHUNK]), 0.0)
        acc = acc + _dot((h * h).astype(BF16), wdown_ref[lo:lo + FF_CHUNK, :])
    gate = jax.nn.sigmoid(_dot(acc.astype(BF16), wpg_ref[...]))
    emb = _rms(_dot(p_ref[...].astype(BF16), wple_ref[...]), gple_ref[...])
    out = acc + gate * emb
    if final:
        out = _rms(out, gfin_ref[...])
    o_ref[...] = out


def _mlp_call(x2d, p2d, lw, g_final, tm, final):
    t = x2d.shape[0]
    row = lambda i: (i, 0)
    kernel = functools.partial(_mlp_kernel, final=final)
    in_specs = [
        pl.BlockSpec((tm, D_MODEL), row),
        pl.BlockSpec((tm, PLE_DIM), row),
        _const_spec((1, D_MODEL)),
        _const_spec((D_MODEL, D_FF)),
        _const_spec((D_FF, D_MODEL)),
        _const_spec((D_MODEL, D_MODEL)),
        _const_spec((PLE_DIM, D_MODEL)),
        _const_spec((1, D_MODEL)),
        _const_spec((1, D_MODEL)),
    ]
    return pl.pallas_call(
        kernel, grid=(t // tm,), in_specs=in_specs,
        out_specs=pl.BlockSpec((tm, D_MODEL), row),
        out_shape=jax.ShapeDtypeStruct((t, D_MODEL), F32),
        compiler_params=pltpu.CompilerParams(dimension_semantics=("parallel",), vmem_limit_bytes=VMEM_LIMIT),
        name="mlp",
    )(x2d, p2d, lw["g_mlp"], lw["w_up"], lw["w_down"], lw["w_pg"], lw["w_ple"], lw["g_ple"], g_final)


def _rot_half_cols(w):
    half = QK_ROPE // 2
    return jnp.concatenate([-w[..., half:], w[..., :half]], axis=-1)


def _layer_weights(i, g_mix, w_in, g_q, w_uq, g_kv, w_ukv, w_oa, conv_w, conv_b, ln_g, ln_b, w_pc, b_gate,
                   w_out, g_mlp, w_up, w_down, w_ple_gate, w_ple, g_ple):
    wi = w_in[i]
    w_kr = wi[:, OFF_KR:OFF_CONV]
    w_kr2 = jnp.concatenate([w_kr, _rot_half_cols(w_kr)], axis=1).T

    wq = w_uq[i].reshape(Q_LORA, N_HEADS, QK_NOPE + QK_ROPE)
    pad = jnp.zeros((Q_LORA, N_HEADS, LANES - QK_NOPE - QK_ROPE), F32)
    wq_plain = jnp.concatenate([wq, pad], axis=-1).reshape(Q_LORA, N_HEADS * LANES)
    wq_rot = jnp.concatenate([jnp.zeros((Q_LORA, N_HEADS, QK_NOPE), F32), _rot_half_cols(wq[..., QK_NOPE:]), pad],
                             axis=-1).reshape(Q_LORA, N_HEADS * LANES)

    wkv = w_ukv[i].reshape(KV_LORA, N_HEADS, QK_NOPE + V_HEAD)
    w_uk = wkv[..., :QK_NOPE].reshape(KV_LORA, N_HEADS * QK_NOPE)
    w_uv = wkv[..., QK_NOPE:].reshape(KV_LORA, N_HEADS * V_HEAD)
    r1 = lambda a: a[i].reshape(1, -1)
    return {
        "g_mix": r1(g_mix), "g_q": r1(g_q), "g_kv": r1(g_kv), "g_mlp": r1(g_mlp), "g_ple": r1(g_ple),
        "b_gate": r1(b_gate), "conv_b": r1(conv_b), "ln_g": r1(ln_g), "ln_b": r1(ln_b),
        "conv_w": conv_w[i],
        "w_lat": wi[:, :OFF_KR].astype(BF16),
        "w_kr": w_kr2.astype(BF16),
        "w_conv": wi[:, OFF_CONV:OFF_GATE].astype(BF16),
        "w_gate": wi[:, OFF_GATE:].astype(BF16),
        "w_uq": jnp.concatenate([wq_plain, wq_rot], axis=1).astype(BF16),
        "w_ukt": w_uk.T.astype(BF16),
        "w_uv": w_uv.astype(BF16),
        "w_oa": w_oa[i].astype(BF16),
        "w_pc": w_pc[i].astype(BF16),
        "w_out": w_out[i].astype(BF16),
        "w_up": w_up[i].astype(BF16),
        "w_down": w_down[i].astype(BF16),
        "w_pg": w_ple_gate[i].astype(BF16),
        "w_ple": w_ple[i].astype(BF16),
    }


def _rope_tables(seq):
    inv = 1.0 / (ROPE_BASE ** (jnp.arange(0, QK_ROPE, 2, dtype=F32) / QK_ROPE))
    ang = jnp.arange(seq, dtype=F32)[:, None] * inv[None, :]
    cos, sin = jnp.cos(ang), jnp.sin(ang)
    scale = ATTN_SCALE * math.log2(math.e)
    ones = jnp.ones((seq, QK_NOPE), F32)
    zpad = jnp.zeros((seq, LANES - QK_NOPE - QK_ROPE), F32)
    cq = jnp.concatenate([ones, cos, cos, zpad], axis=1) * scale
    sq = jnp.concatenate([jnp.zeros((seq, QK_NOPE), F32), sin, sin, zpad], axis=1) * scale
    ckt = jnp.concatenate([cos, cos], axis=1).T
    skt = jnp.concatenate([sin, sin], axis=1).T
    return {"cq": cq, "sq": sq, "ckt": ckt, "skt": skt}


def _pick_tile(n, target):
    tile = min(n, target)
    assert n % tile == 0, (n, tile)
    return tile


def _run_group(x, p, layers, g_final):
    batch, seq, _ = x.shape
    depth = p.shape[0]
    tm = _pick_tile(seq, 512)
    tq = _pick_tile(seq, 512)
    tk = _pick_tile(seq // 2, 1024)
    tabs = _rope_tables(seq)
    x2d = x.reshape(batch * seq, D_MODEL)
    p3d = p.reshape(depth, batch * seq, PLE_DIM)
    gfin = g_final.reshape(1, D_MODEL)
    for i in range(depth):
        lw = layers[i]
        q, kt, v, glu = _pre_call(x2d, lw, tabs, batch, seq, tm)
        attn = _attn_call(q, kt, v, batch, seq, tq, tk)
        x2d = _merge_call(x2d, attn, glu, lw, batch, seq, tm)
        x2d = _mlp_call(x2d, p3d[i], lw, gfin, tm, final=(i == depth - 1))
    return x2d.reshape(batch, seq, D_MODEL)


def kernel(x_prompt, x_sample, p_prompt, p_sample, g_mix, w_in, g_q, w_uq, g_kv, w_ukv, w_oa, conv_w, conv_b,
           ln_g, ln_b, w_pc, b_gate, w_out, g_mlp, w_up, w_down, w_ple_gate, w_ple, g_ple, g_final):
    depth = w_in.shape[0]
    layers = [_layer_weights(i, g_mix, w_in, g_q, w_uq, g_kv, w_ukv, w_oa, conv_w, conv_b, ln_g, ln_b, w_pc,
                             b_gate, w_out, g_mlp, w_up, w_down, w_ple_gate, w_ple, g_ple) for i in range(depth)]
    y_prompt = _run_group(x_prompt, p_prompt, layers, g_final)
    y_sample = _run_group(x_sample, p_sample, layers, g_final)
    return (y_prompt, y_sample)
```

```python
import functools
import math

import jax
import jax.numpy as jnp
from jax import lax
from jax.experimental import pallas as pl
from jax.experimental.pallas import tpu as pltpu

D_MODEL = 1024
N_HEADS = 8
Q_LORA = 256
KV_LORA = 128
QK_NOPE = 64
QK_ROPE = 32
V_HEAD = 64
ROPE_BASE = 10000.0
ATTN_SCALE = (QK_NOPE + QK_ROPE) ** -0.5
CONV_CH = 512
CONV_W = 31
CONV_PAD = (CONV_W - 1) // 2
D_FF = 4 * D_MODEL
PLE_DIM = 256
EPS = 1e-6

OFF_KV = Q_LORA
OFF_KR = OFF_KV + KV_LORA
OFF_CONV = OFF_KR + QK_ROPE
OFF_GATE = OFF_CONV + 2 * CONV_CH

LANES = 128
SUBLANES = 8
HALO = 16
VMEM_LIMIT = 56 * 1024 * 1024
FF_CHUNK = 1024
ROW_TILE = 512
Q_TILE = 512
KV_CHUNK = 2048

F32 = jnp.float32
BF16 = jnp.bfloat16


def _rms(x, g):
    return x * lax.rsqrt(jnp.mean(x * x, axis=-1, keepdims=True) + EPS) * g


def _dot(a, b):
    return jnp.dot(a, b, preferred_element_type=F32)


def _dot_nt(a, b):
    return lax.dot_general(a, b, (((1,), (1,)), ((), ())), preferred_element_type=F32)


def _const_spec(shape):
    return pl.BlockSpec(shape, lambda *_: (0,) * len(shape), pipeline_mode=pl.Buffered(1))


def _pre_kernel(x_ref, gmix_ref, wlat_ref, wconv_ref, gq_ref, wuq_ref, gkv_ref, wukt_ref,
                wuv_ref, cq_ref, sq_ref, ckt_ref, skt_ref, q_ref, kt_ref, v_ref, glu_ref):
    x = x_ref[...]
    u = _rms(x, gmix_ref[...]).astype(BF16)
    lat = _dot(u, wlat_ref[...])
    cqn = _rms(lat[:, :Q_LORA], gq_ref[...]).astype(BF16)
    ckvn = _rms(lat[:, Q_LORA:Q_LORA + KV_LORA], gkv_ref[...]).astype(BF16)
    kr2 = lat[:, Q_LORA + KV_LORA:].T

    q2 = _dot(cqn, wuq_ref[...])
    cq_t, sq_t = cq_ref[...], sq_ref[...]
    hw = N_HEADS * LANES
    for h in range(N_HEADS):
        lo = h * LANES
        q_ref[:, lo:lo + LANES] = (q2[:, lo:lo + LANES] * cq_t
                                   + q2[:, hw + lo:hw + lo + LANES] * sq_t).astype(BF16)

    v = _dot(ckvn, wuv_ref[...]).astype(BF16)
    ones = jnp.ones((v.shape[0], LANES), BF16)
    for g in range(N_HEADS // 2):
        v_ref[:, 2 * g * LANES:(2 * g + 1) * LANES] = v[:, g * LANES:(g + 1) * LANES]
        v_ref[:, (2 * g + 1) * LANES:(2 * g + 2) * LANES] = ones

    knt = _dot_nt(wukt_ref[...], ckvn)
    krt = (kr2[:QK_ROPE] * ckt_ref[...] + kr2[QK_ROPE:2 * QK_ROPE] * skt_ref[...]).astype(BF16)
    zeros = jnp.zeros((LANES - QK_NOPE - QK_ROPE, krt.shape[1]), BF16)
    for h in range(N_HEADS):
        kt_ref[0, h, 0:QK_NOPE, :] = knt[h * QK_NOPE:(h + 1) * QK_NOPE].astype(BF16)
        kt_ref[0, h, QK_NOPE:QK_NOPE + QK_ROPE, :] = krt
        kt_ref[0, h, QK_NOPE + QK_ROPE:, :] = zeros

    zc = _dot(u, wconv_ref[...])
    glu_ref[...] = zc[:, :CONV_CH] * jax.nn.sigmoid(zc[:, CONV_CH:])


def _pre_call(x2d, lw, tabs, batch, seq, tm):
    t = batch * seq
    tiles_per_seq = seq // tm
    row = lambda i: (i, 0)
    in_specs = [
        pl.BlockSpec((tm, D_MODEL), row),
        _const_spec((1, D_MODEL)),
        _const_spec((D_MODEL, Q_LORA + KV_LORA + LANES)),
        _const_spec((D_MODEL, 2 * CONV_CH)),
        _const_spec((1, Q_LORA)),
        _const_spec((Q_LORA, 2 * N_HEADS * LANES)),
        _const_spec((1, KV_LORA)),
        _const_spec((N_HEADS * QK_NOPE, KV_LORA)),
        _const_spec((KV_LORA, N_HEADS * V_HEAD)),
        pl.BlockSpec((tm, LANES), lambda i: (i % tiles_per_seq, 0)),
        pl.BlockSpec((tm, LANES), lambda i: (i % tiles_per_seq, 0)),
        pl.BlockSpec((QK_ROPE, tm), lambda i: (0, i % tiles_per_seq)),
        pl.BlockSpec((QK_ROPE, tm), lambda i: (0, i % tiles_per_seq)),
    ]
    out_specs = [
        pl.BlockSpec((tm, N_HEADS * LANES), row),
        pl.BlockSpec((1, N_HEADS, LANES, tm), lambda i: (i // tiles_per_seq, 0, 0, i % tiles_per_seq)),
        pl.BlockSpec((tm, N_HEADS * LANES), row),
        pl.BlockSpec((tm, CONV_CH), row),
    ]
    out_shape = [
        jax.ShapeDtypeStruct((t, N_HEADS * LANES), BF16),
        jax.ShapeDtypeStruct((batch, N_HEADS, LANES, seq), BF16),
        jax.ShapeDtypeStruct((t, N_HEADS * LANES), BF16),
        jax.ShapeDtypeStruct((t, CONV_CH), F32),
    ]
    return pl.pallas_call(
        _pre_kernel, grid=(t // tm,), in_specs=in_specs, out_specs=out_specs, out_shape=out_shape,
        compiler_params=pltpu.CompilerParams(dimension_semantics=("parallel",), vmem_limit_bytes=VMEM_LIMIT),
        name="pre",
    )(x2d, lw["g_mix"], lw["w_lat"], lw["w_conv"], lw["g_q"], lw["w_uq"], lw["g_kv"],
      lw["w_ukt"], lw["w_uv"], tabs["cq"], tabs["sq"], tabs["ckt"], tabs["skt"])


def _attn_kernel(q_ref, kt_ref, v_ref, o_ref, s_sc, m_sc, ml_sc, acc_sc, *, tk, n_chunks):
    tq = q_ref.shape[0]
    reps = tk // LANES

    def chunk_start(c):
        return c * tk

    outs = []
    for hh in range(2):
        q = q_ref[:, hh * LANES:(hh + 1) * LANES]

        def stage_a(c, slot, q=q, hh=hh):
            s = _dot(q, kt_ref[0, hh, :, pl.ds(chunk_start(c), tk)])
            s_sc[slot] = s
            ml_sc[slot] = jnp.broadcast_to(jnp.max(s, axis=1, keepdims=True), (tq, LANES))

        def stage_b(c, slot):
            m_prev = m_sc[...]
            m_next = jnp.maximum(m_prev, ml_sc[slot])
            p = jnp.exp2(s_sc[slot] - jnp.tile(m_next, (1, reps)))
            alpha = jnp.exp2(m_prev - m_next)
            pv = _dot(p.astype(BF16), v_ref[pl.ds(chunk_start(c), tk), :])
            acc_sc[...] = jnp.tile(alpha, (1, 2)) * acc_sc[...] + pv
            m_sc[...] = m_next

        m_sc[...] = jnp.full(m_sc.shape, -jnp.inf, F32)
        acc_sc[...] = jnp.zeros(acc_sc.shape, F32)
        stage_a(0, 0)
        for c in range(n_chunks):
            if c + 1 < n_chunks:
                stage_a(c + 1, (c + 1) % 2)
            stage_b(c, c % 2)
        acc = acc_sc[...]
        outs.append(acc[:, :LANES] / acc[:, LANES:])
    lane = lax.broadcasted_iota(jnp.int32, (tq, LANES), 1)
    o_ref[...] = jnp.where(lane < V_HEAD, outs[0], outs[1]).astype(o_ref.dtype)


def _attn_call(q, kt, v, batch, seq, tq, tk):
    t = batch * seq
    nq = seq // tq
    pairs = N_HEADS // 2
    n_chunks = seq // tk
    kernel = functools.partial(_attn_kernel, tk=tk, n_chunks=n_chunks)
    return pl.pallas_call(
        kernel,
        grid=(batch, pairs, nq),
        in_specs=[
            pl.BlockSpec((tq, 2 * LANES), lambda b, g, i: (b * nq + i, g)),
            pl.BlockSpec((1, 2, LANES, seq), lambda b, g, i: (b, g, 0, 0)),
            pl.BlockSpec((seq, 2 * LANES), lambda b, g, i: (b, g)),
        ],
        out_specs=pl.BlockSpec((tq, 2 * V_HEAD), lambda b, g, i: (b * nq + i, g)),
        out_shape=jax.ShapeDtypeStruct((t, N_HEADS * V_HEAD), BF16),
        scratch_shapes=[pltpu.VMEM((2, tq, tk), F32), pltpu.VMEM((tq, LANES), F32),
                        pltpu.VMEM((2, tq, LANES), F32), pltpu.VMEM((tq, 2 * LANES), F32)],
        compiler_params=pltpu.CompilerParams(dimension_semantics=("parallel", "parallel", "arbitrary"),
                                             vmem_limit_bytes=VMEM_LIMIT),
        name="attn",
    )(q, kt, v)


def _merge_kernel(x_ref, attn_ref, gprev_ref, gcur_ref, gnext_ref, gmix_ref, wgate_ref, bgate_ref, woa_ref,
                  convw_ref, convb_ref, lng_ref, lnb_ref, wpc_ref, wout_ref, o_ref, gext_sc, *, tiles_per_seq):
    tm = x_ref.shape[0]
    i = pl.program_id(0)
    x = x_ref[...]
    u = _rms(x, gmix_ref[...]).astype(BF16)
    gates = jax.nn.sigmoid(_dot(u, wgate_ref[...]) + bgate_ref[...])
    attn = _dot(attn_ref[...], woa_ref[...])

    pos = i % tiles_per_seq
    gext_sc[0:HALO, :] = jnp.where(pos == 0, 0.0, gprev_ref[...])
    gext_sc[HALO:HALO + tm, :] = gcur_ref[...]
    gext_sc[HALO + tm:, :] = jnp.where(pos == tiles_per_seq - 1, 0.0, gnext_ref[...])
    y = jnp.broadcast_to(convb_ref[...], (tm, CONV_CH))
    for r in range(SUBLANES):
        part = None
        for k in range(CONV_W):
            off = HALO - CONV_PAD + k
            if off % SUBLANES != r:
                continue
            base = off - r
            term = convw_ref[k:k + 1, :] * gext_sc[base:base + tm + SUBLANES, :]
            part = term if part is None else part + term
        if part is not None:
            y = y + part[r:r + tm, :]
    mu = jnp.mean(y, axis=-1, keepdims=True)
    yc = y - mu
    var = jnp.mean(yc * yc, axis=-1, keepdims=True)
    yn = yc * lax.rsqrt(var + EPS) * lng_ref[...] + lnb_ref[...]
    conv = _dot((yn * jax.nn.sigmoid(yn)).astype(BF16), wpc_ref[...])

    merged = gates[:, :D_MODEL] * attn + gates[:, D_MODEL:] * conv
    o_ref[...] = x + _dot(merged.astype(BF16), wout_ref[...])


def _merge_call(x2d, attn, glu, lw, batch, seq, tm):
    t = batch * seq
    tiles_per_seq = seq // tm
    hb = tm // HALO
    n_hblk = t // HALO
    row = lambda i: (i, 0)
    kernel = functools.partial(_merge_kernel, tiles_per_seq=tiles_per_seq)
    in_specs = [
        pl.BlockSpec((tm, D_MODEL), row),
        pl.BlockSpec((tm, N_HEADS * V_HEAD), row),
        pl.BlockSpec((HALO, CONV_CH), lambda i: (jnp.maximum(i * hb - 1, 0), 0)),
        pl.BlockSpec((tm, CONV_CH), row),
        pl.BlockSpec((HALO, CONV_CH), lambda i: (jnp.minimum((i + 1) * hb, n_hblk - 1), 0)),
        _const_spec((1, D_MODEL)),
        _const_spec((D_MODEL, 2 * D_MODEL)),
        _const_spec((1, 2 * D_MODEL)),
        _const_spec((N_HEADS * V_HEAD, D_MODEL)),
        _const_spec((CONV_W, CONV_CH)),
        _const_spec((1, CONV_CH)),
        _const_spec((1, CONV_CH)),
        _const_spec((1, CONV_CH)),
        _const_spec((CONV_CH, D_MODEL)),
        _const_spec((D_MODEL, D_MODEL)),
    ]
    return pl.pallas_call(
        kernel, grid=(t // tm,), in_specs=in_specs,
        out_specs=pl.BlockSpec((tm, D_MODEL), row),
        out_shape=jax.ShapeDtypeStruct((t, D_MODEL), F32),
        scratch_shapes=[pltpu.VMEM((tm + 2 * HALO, CONV_CH), F32)],
        compiler_params=pltpu.CompilerParams(dimension_semantics=("parallel",), vmem_limit_bytes=VMEM_LIMIT),
        name="merge",
    )(x2d, attn, glu, glu, glu, lw["g_mix"], lw["w_gate"], lw["b_gate"], lw["w_oa"], lw["conv_w"],
      lw["conv_b"], lw["ln_g"], lw["ln_b"], lw["w_pc"], lw["w_out"])


def _mlp_kernel(x_ref, p_ref, gmlp_ref, wup_ref, wdown_ref, wpg_ref, wple_ref, gple_ref, gfin_ref, o_ref, *,
                final):
    x = x_ref[...]
    un = _rms(x, gmlp_ref[...]).astype(BF16)
    acc = x
    for c in range(D_FF // FF_CHUNK):
        lo = c * FF_CHUNK
        h = jnp.maximum(_dot(un, wup_ref[:, lo:lo + FF_CHUNK]), 0.0)
        acc = acc + _dot((h * h).astype(BF16), wdown_ref[lo:lo + FF_CHUNK, :])
    gate = jax.nn.sigmoid(_dot(acc.astype(BF16), wpg_ref[...]))
    emb = _rms(_dot(p_ref[...].astype(BF16), wple_ref[...]), gple_ref[...])
    out = acc + gate * emb
    if final:
        out = _rms(out, gfin_ref[...])
    o_ref[...] = out


def _mlp_call(x2d, p3d, layer, lw, g_final, tm, final):
    t = x2d.shape[0]
    row = lambda i: (i, 0)
    kernel = functools.partial(_mlp_kernel, final=final)
    in_specs = [
        pl.BlockSpec((tm, D_MODEL), row),
        pl.BlockSpec((None, tm, PLE_DIM), lambda i: (layer, i, 0)),
        _const_spec((1, D_MODEL)),
        _const_spec((D_MODEL, D_FF)),
        _const_spec((D_FF, D_MODEL)),
        _const_spec((D_MODEL, D_MODEL)),
        _const_spec((PLE_DIM, D_MODEL)),
        _const_spec((1, D_MODEL)),
        _const_spec((1, D_MODEL)),
    ]
    return pl.pallas_call(
        kernel, grid=(t // tm,), in_specs=in_specs,
        out_specs=pl.BlockSpec((tm, D_MODEL), row),
        out_shape=jax.ShapeDtypeStruct((t, D_MODEL), F32),
        compiler_params=pltpu.CompilerParams(dimension_semantics=("parallel",), vmem_limit_bytes=VMEM_LIMIT),
        name="mlp",
    )(x2d, p3d, lw["g_mlp"], lw["w_up"], lw["w_down"], lw["w_pg"], lw["w_ple"], lw["g_ple"], g_final)


def _rot_half_cols(w):
    half = QK_ROPE // 2
    return jnp.concatenate([-w[..., half:], w[..., :half]], axis=-1)


def _layer_weights(i, g_mix, w_in, g_q, w_uq, g_kv, w_ukv, w_oa, conv_w, conv_b, ln_g, ln_b, w_pc, b_gate,
                   w_out, g_mlp, w_up, w_down, w_ple_gate, w_ple, g_ple):
    wi = w_in[i]
    w_kr = wi[:, OFF_KR:OFF_CONV]
    w_kr2 = jnp.pad(jnp.concatenate([w_kr, _rot_half_cols(w_kr)], axis=1), ((0, 0), (0, LANES - 2 * QK_ROPE)))

    wq = w_uq[i].reshape(Q_LORA, N_HEADS, QK_NOPE + QK_ROPE)
    pad = jnp.zeros((Q_LORA, N_HEADS, LANES - QK_NOPE - QK_ROPE), F32)
    wq_plain = jnp.concatenate([wq, pad], axis=-1).reshape(Q_LORA, N_HEADS * LANES)
    wq_rot = jnp.concatenate([jnp.zeros((Q_LORA, N_HEADS, QK_NOPE), F32), _rot_half_cols(wq[..., QK_NOPE:]), pad],
                             axis=-1).reshape(Q_LORA, N_HEADS * LANES)

    wkv = w_ukv[i].reshape(KV_LORA, N_HEADS, QK_NOPE + V_HEAD)
    w_uk = wkv[..., :QK_NOPE].reshape(KV_LORA, N_HEADS * QK_NOPE)
    w_uv = wkv[..., QK_NOPE:].reshape(KV_LORA, N_HEADS * V_HEAD)
    r1 = lambda a: a[i].reshape(1, -1)
    return {
        "g_mix": r1(g_mix), "g_q": r1(g_q), "g_kv": r1(g_kv), "g_mlp": r1(g_mlp), "g_ple": r1(g_ple),
        "b_gate": r1(b_gate), "conv_b": r1(conv_b), "ln_g": r1(ln_g), "ln_b": r1(ln_b),
        "conv_w": conv_w[i],
        "w_lat": jnp.concatenate([wi[:, :OFF_KR], w_kr2], axis=1).astype(BF16),
        "w_conv": wi[:, OFF_CONV:OFF_GATE].astype(BF16),
        "w_gate": wi[:, OFF_GATE:].astype(BF16),
        "w_uq": jnp.concatenate([wq_plain, wq_rot], axis=1).astype(BF16),
        "w_ukt": w_uk.T.astype(BF16),
        "w_uv": w_uv.astype(BF16),
        "w_oa": w_oa[i].astype(BF16),
        "w_pc": w_pc[i].astype(BF16),
        "w_out": w_out[i].astype(BF16),
        "w_up": w_up[i].astype(BF16),
        "w_down": w_down[i].astype(BF16),
        "w_pg": w_ple_gate[i].astype(BF16),
        "w_ple": w_ple[i].astype(BF16),
    }


def _rope_tables(seq):
    inv = 1.0 / (ROPE_BASE ** (jnp.arange(0, QK_ROPE, 2, dtype=F32) / QK_ROPE))
    ang = jnp.arange(seq, dtype=F32)[:, None] * inv[None, :]
    cos, sin = jnp.cos(ang), jnp.sin(ang)
    scale = ATTN_SCALE * math.log2(math.e)
    ones = jnp.ones((seq, QK_NOPE), F32)
    zpad = jnp.zeros((seq, LANES - QK_NOPE - QK_ROPE), F32)
    cq = jnp.concatenate([ones, cos, cos, zpad], axis=1) * scale
    sq = jnp.concatenate([jnp.zeros((seq, QK_NOPE), F32), sin, sin, zpad], axis=1) * scale
    ckt = jnp.concatenate([cos, cos], axis=1).T
    skt = jnp.concatenate([sin, sin], axis=1).T
    return {"cq": cq, "sq": sq, "ckt": ckt, "skt": skt}


def _pick_tile(n, target):
    tile = min(n, target)
    assert n % tile == 0, (n, tile)
    return tile


def _run_group(x, p, layers, g_final):
    batch, seq, _ = x.shape
    depth = p.shape[0]
    tm = _pick_tile(seq, ROW_TILE)
    tq = _pick_tile(seq, Q_TILE)
    tk = _pick_tile(seq // 2, KV_CHUNK)
    tabs = _rope_tables(seq)
    x2d = x.reshape(batch * seq, D_MODEL)
    p3d = p.reshape(depth, batch * seq, PLE_DIM)
    gfin = g_final.reshape(1, D_MODEL)
    for i in range(depth):
        lw = layers[i]
        q, kt, v, glu = _pre_call(x2d, lw, tabs, batch, seq, tm)
        attn = _attn_call(q, kt, v, batch, seq, tq, tk)
        x2d = _merge_call(x2d, attn, glu, lw, batch, seq, tm)
        x2d = _mlp_call(x2d, p3d, i, lw, gfin, tm, final=(i == depth - 1))
    return x2d.reshape(batch, seq, D_MODEL)


def kernel(x_prompt, x_sample, p_prompt, p_sample, g_mix, w_in, g_q, w_uq, g_kv, w_ukv, w_oa, conv_w, conv_b,
           ln_g, ln_b, w_pc, b_gate, w_out, g_mlp, w_up, w_down, w_ple_gate, w_ple, g_ple, g_final):
    depth = w_in.shape[0]
    layers = [_layer_weights(i, g_mix, w_in, g_q, w_uq, g_kv, w_ukv, w_oa, conv_w, conv_b, ln_g, ln_b, w_pc,
                             b_gate, w_out, g_mlp, w_up, w_down, w_ple_gate, w_ple, g_ple) for i in range(depth)]
    y_prompt = _run_group(x_prompt, p_prompt, layers, g_final)
    y_sample = _run_group(x_sample, p_sample, layers, g_final)
    return (y_prompt, y_sample)
```

```python
import functools
import math

import jax
import jax.numpy as jnp
from jax import lax
from jax.experimental import pallas as pl
from jax.experimental.pallas import tpu as pltpu

D_MODEL = 1024
N_HEADS = 8
Q_LORA = 256
KV_LORA = 128
QK_NOPE = 64
QK_ROPE = 32
V_HEAD = 64
ROPE_BASE = 10000.0
ATTN_SCALE = (QK_NOPE + QK_ROPE) ** -0.5
CONV_CH = 512
CONV_W = 31
CONV_PAD = (CONV_W - 1) // 2
D_FF = 4 * D_MODEL
PLE_DIM = 256
EPS = 1e-6

OFF_KV = Q_LORA
OFF_KR = OFF_KV + KV_LORA
OFF_CONV = OFF_KR + QK_ROPE
OFF_GATE = OFF_CONV + 2 * CONV_CH

LANES = 128
SUBLANES = 8
HALO = 16
VMEM_LIMIT = 56 * 1024 * 1024
FF_CHUNK = 1024
ROW_TILE = 512
PRE_TILE = 1024
Q_TILE = 512
KV_CHUNK = 2048

F32 = jnp.float32
BF16 = jnp.bfloat16


def _rms(x, g):
    return x * lax.rsqrt(jnp.mean(x * x, axis=-1, keepdims=True) + EPS) * g


def _sigmoid_tanh(x):
    return 0.5 * jnp.tanh(0.5 * x) + 0.5


def _dot(a, b):
    return jnp.dot(a, b, preferred_element_type=F32)


def _dot_nt(a, b):
    return lax.dot_general(a, b, (((1,), (1,)), ((), ())), preferred_element_type=F32)


def _const_spec(shape):
    return pl.BlockSpec(shape, lambda *_: (0,) * len(shape), pipeline_mode=pl.Buffered(1))


def _pre_kernel(x_ref, gmix_ref, wlat_ref, wconv_ref, gq_ref, wuq_ref, gkv_ref, wukt_ref,
                wuv_ref, cq_ref, sq_ref, ckt_ref, skt_ref, q_ref, kt_ref, v_ref, glu_ref):
    x = x_ref[...]
    u = _rms(x, gmix_ref[...]).astype(BF16)
    lat = _dot(u, wlat_ref[...])
    cqn = _rms(lat[:, :Q_LORA], gq_ref[...]).astype(BF16)
    ckvn = _rms(lat[:, Q_LORA:Q_LORA + KV_LORA], gkv_ref[...]).astype(BF16)
    kr2 = lat[:, Q_LORA + KV_LORA:].T

    q2 = _dot(cqn, wuq_ref[...])
    cq_t, sq_t = cq_ref[...], sq_ref[...]
    hw = N_HEADS * LANES
    for h in range(N_HEADS):
        lo = h * LANES
        q_ref[:, lo:lo + LANES] = (q2[:, lo:lo + LANES] * cq_t
                                   + q2[:, hw + lo:hw + lo + LANES] * sq_t).astype(BF16)

    v = _dot(ckvn, wuv_ref[...]).astype(BF16)
    ones = jnp.ones((v.shape[0], LANES), BF16)
    for g in range(N_HEADS // 2):
        v_ref[:, 2 * g * LANES:(2 * g + 1) * LANES] = v[:, g * LANES:(g + 1) * LANES]
        v_ref[:, (2 * g + 1) * LANES:(2 * g + 2) * LANES] = ones

    knt = _dot_nt(wukt_ref[...], ckvn)
    krt = (kr2[:QK_ROPE] * ckt_ref[...] + kr2[QK_ROPE:2 * QK_ROPE] * skt_ref[...]).astype(BF16)
    zeros = jnp.zeros((LANES - QK_NOPE - QK_ROPE, krt.shape[1]), BF16)
    for h in range(N_HEADS):
        kt_ref[0, h, 0:QK_NOPE, :] = knt[h * QK_NOPE:(h + 1) * QK_NOPE].astype(BF16)
        kt_ref[0, h, QK_NOPE:QK_NOPE + QK_ROPE, :] = krt
        kt_ref[0, h, QK_NOPE + QK_ROPE:, :] = zeros

    zc = _dot(u, wconv_ref[...])
    glu_ref[...] = zc[:, :CONV_CH] * _sigmoid_tanh(zc[:, CONV_CH:])


def _pre_call(x2d, lw, tabs, batch, seq, tm):
    t = batch * seq
    tiles_per_seq = seq // tm
    row = lambda i: (i, 0)
    in_specs = [
        pl.BlockSpec((tm, D_MODEL), row),
        _const_spec((1, D_MODEL)),
        _const_spec((D_MODEL, Q_LORA + KV_LORA + LANES)),
        _const_spec((D_MODEL, 2 * CONV_CH)),
        _const_spec((1, Q_LORA)),
        _const_spec((Q_LORA, 2 * N_HEADS * LANES)),
        _const_spec((1, KV_LORA)),
        _const_spec((N_HEADS * QK_NOPE, KV_LORA)),
        _const_spec((KV_LORA, N_HEADS * V_HEAD)),
        pl.BlockSpec((tm, LANES), lambda i: (i % tiles_per_seq, 0)),
        pl.BlockSpec((tm, LANES), lambda i: (i % tiles_per_seq, 0)),
        pl.BlockSpec((QK_ROPE, tm), lambda i: (0, i % tiles_per_seq)),
        pl.BlockSpec((QK_ROPE, tm), lambda i: (0, i % tiles_per_seq)),
    ]
    out_specs = [
        pl.BlockSpec((tm, N_HEADS * LANES), row),
        pl.BlockSpec((1, N_HEADS, LANES, tm), lambda i: (i // tiles_per_seq, 0, 0, i % tiles_per_seq)),
        pl.BlockSpec((tm, N_HEADS * LANES), row),
        pl.BlockSpec((tm, CONV_CH), row),
    ]
    out_shape = [
        jax.ShapeDtypeStruct((t, N_HEADS * LANES), BF16),
        jax.ShapeDtypeStruct((batch, N_HEADS, LANES, seq), BF16),
        jax.ShapeDtypeStruct((t, N_HEADS * LANES), BF16),
        jax.ShapeDtypeStruct((t, CONV_CH), F32),
    ]
    return pl.pallas_call(
        _pre_kernel, grid=(t // tm,), in_specs=in_specs, out_specs=out_specs, out_shape=out_shape,
        compiler_params=pltpu.CompilerParams(dimension_semantics=("parallel",), vmem_limit_bytes=VMEM_LIMIT),
        name="pre",
    )(x2d, lw["g_mix"], lw["w_lat"], lw["w_conv"], lw["g_q"], lw["w_uq"], lw["g_kv"],
      lw["w_ukt"], lw["w_uv"], tabs["cq"], tabs["sq"], tabs["ckt"], tabs["skt"])


def _attn_kernel(q_ref, kt_ref, v_ref, o_ref, s_sc, m_sc, ml_sc, acc_sc, *, tk, n_chunks):
    tq = q_ref.shape[0]
    reps = tk // LANES

    def chunk_start(c):
        return c * tk

    outs = []
    for hh in range(2):
        q = q_ref[:, hh * LANES:(hh + 1) * LANES]

        def stage_a(c, slot, q=q, hh=hh):
            s = _dot(q, kt_ref[0, hh, :, pl.ds(chunk_start(c), tk)])
            s_sc[slot] = s
            ml_sc[slot] = jnp.broadcast_to(jnp.max(s, axis=1, keepdims=True), (tq, LANES))

        def stage_b(c, slot):
            m_prev = m_sc[...]
            m_next = jnp.maximum(m_prev, ml_sc[slot])
            p = jnp.exp2(s_sc[slot] - jnp.tile(m_next, (1, reps)))
            alpha = jnp.exp2(m_prev - m_next)
            pv = _dot(p.astype(BF16), v_ref[pl.ds(chunk_start(c), tk), :])
            acc_sc[...] = jnp.tile(alpha, (1, 2)) * acc_sc[...] + pv
            m_sc[...] = m_next

        m_sc[...] = jnp.full(m_sc.shape, -jnp.inf, F32)
        acc_sc[...] = jnp.zeros(acc_sc.shape, F32)
        stage_a(0, 0)
        for c in range(n_chunks):
            if c + 1 < n_chunks:
                stage_a(c + 1, (c + 1) % 2)
            stage_b(c, c % 2)
        acc = acc_sc[...]
        outs.append(acc[:, :LANES] / acc[:, LANES:])
    lane = lax.broadcasted_iota(jnp.int32, (tq, LANES), 1)
    o_ref[...] = jnp.where(lane < V_HEAD, outs[0], outs[1]).astype(o_ref.dtype)


def _attn_call(q, kt, v, batch, seq, tq, tk):
    t = batch * seq
    nq = seq // tq
    pairs = N_HEADS // 2
    n_chunks = seq // tk
    kernel = functools.partial(_attn_kernel, tk=tk, n_chunks=n_chunks)
    return pl.pallas_call(
        kernel,
        grid=(batch, pairs, nq),
        in_specs=[
            pl.BlockSpec((tq, 2 * LANES), lambda b, g, i: (b * nq + i, g)),
            pl.BlockSpec((1, 2, LANES, seq), lambda b, g, i: (b, g, 0, 0)),
            pl.BlockSpec((seq, 2 * LANES), lambda b, g, i: (b, g)),
        ],
        out_specs=pl.BlockSpec((tq, 2 * V_HEAD), lambda b, g, i: (b * nq + i, g)),
        out_shape=jax.ShapeDtypeStruct((t, N_HEADS * V_HEAD), BF16),
        scratch_shapes=[pltpu.VMEM((2, tq, tk), F32), pltpu.VMEM((tq, LANES), F32),
                        pltpu.VMEM((2, tq, LANES), F32), pltpu.VMEM((tq, 2 * LANES), F32)],
        compiler_params=pltpu.CompilerParams(dimension_semantics=("parallel", "parallel", "arbitrary"),
                                             vmem_limit_bytes=VMEM_LIMIT),
        name="attn",
    )(q, kt, v)


def _merge_kernel(x_ref, attn_ref, gprev_ref, gcur_ref, gnext_ref, gmix_ref, wgate_ref, bgate_ref, woa_ref,
                  convw_ref, convb_ref, lng_ref, lnb_ref, wpc_ref, wout_ref, o_ref, gext_sc, *, tiles_per_seq):
    tm = x_ref.shape[0]
    i = pl.program_id(0)
    x = x_ref[...]
    u = _rms(x, gmix_ref[...]).astype(BF16)
    gates = jax.nn.sigmoid(_dot(u, wgate_ref[...]) + bgate_ref[...])
    attn = _dot(attn_ref[...], woa_ref[...])

    pos = i % tiles_per_seq
    gext_sc[0:HALO, :] = jnp.where(pos == 0, 0.0, gprev_ref[...])
    gext_sc[HALO:HALO + tm, :] = gcur_ref[...]
    gext_sc[HALO + tm:, :] = jnp.where(pos == tiles_per_seq - 1, 0.0, gnext_ref[...])
    y = jnp.broadcast_to(convb_ref[...], (tm, CONV_CH))
    for r in range(SUBLANES):
        part = None
        for k in range(CONV_W):
            off = HALO - CONV_PAD + k
            if off % SUBLANES != r:
                continue
            base = off - r
            term = convw_ref[k:k + 1, :] * gext_sc[base:base + tm + SUBLANES, :]
            part = term if part is None else part + term
        if part is not None:
            y = y + part[r:r + tm, :]
    mu = jnp.mean(y, axis=-1, keepdims=True)
    yc = y - mu
    var = jnp.mean(yc * yc, axis=-1, keepdims=True)
    yn = yc * lax.rsqrt(var + EPS) * lng_ref[...] + lnb_ref[...]
    conv = _dot((yn * jax.nn.sigmoid(yn)).astype(BF16), wpc_ref[...])

    merged = gates[:, :D_MODEL] * attn + gates[:, D_MODEL:] * conv
    o_ref[...] = x + _dot(merged.astype(BF16), wout_ref[...])


def _merge_call(x2d, attn, glu, lw, batch, seq, tm):
    t = batch * seq
    tiles_per_seq = seq // tm
    hb = tm // HALO
    n_hblk = t // HALO
    row = lambda i: (i, 0)
    kernel = functools.partial(_merge_kernel, tiles_per_seq=tiles_per_seq)
    in_specs = [
        pl.BlockSpec((tm, D_MODEL), row),
        pl.BlockSpec((tm, N_HEADS * V_HEAD), row),
        pl.BlockSpec((HALO, CONV_CH), lambda i: (jnp.maximum(i * hb - 1, 0), 0)),
        pl.BlockSpec((tm, CONV_CH), row),
        pl.BlockSpec((HALO, CONV_CH), lambda i: (jnp.minimum((i + 1) * hb, n_hblk - 1), 0)),
        _const_spec((1, D_MODEL)),
        _const_spec((D_MODEL, 2 * D_MODEL)),
        _const_spec((1, 2 * D_MODEL)),
        _const_spec((N_HEADS * V_HEAD, D_MODEL)),
        _const_spec((CONV_W, CONV_CH)),
        _const_spec((1, CONV_CH)),
        _const_spec((1, CONV_CH)),
        _const_spec((1, CONV_CH)),
        _const_spec((CONV_CH, D_MODEL)),
        _const_spec((D_MODEL, D_MODEL)),
    ]
    return pl.pallas_call(
        kernel, grid=(t // tm,), in_specs=in_specs,
        out_specs=pl.BlockSpec((tm, D_MODEL), row),
        out_shape=jax.ShapeDtypeStruct((t, D_MODEL), F32),
        scratch_shapes=[pltpu.VMEM((tm + 2 * HALO, CONV_CH), F32)],
        compiler_params=pltpu.CompilerParams(dimension_semantics=("parallel",), vmem_limit_bytes=VMEM_LIMIT),
        name="merge",
    )(x2d, attn, glu, glu, glu, lw["g_mix"], lw["w_gate"], lw["b_gate"], lw["w_oa"], lw["conv_w"],
      lw["conv_b"], lw["ln_g"], lw["ln_b"], lw["w_pc"], lw["w_out"])


def _mlp_kernel(x_ref, p_ref, gmlp_ref, wup_ref, wdown_ref, wpg_ref, wple_ref, gple_ref, gfin_ref, o_ref, *,
                final):
    x = x_ref[...]
    un = _rms(x, gmlp_ref[...]).astype(BF16)
    acc = x
    for c in range(D_FF // FF_CHUNK):
        lo = c * FF_CHUNK
        h = jnp.maximum(_dot(un, wup_ref[:, lo:lo + FF_CHUNK]), 0.0)
        acc = acc + _dot((h * h).astype(BF16), wdown_ref[lo:lo + FF_CHUNK, :])
    gate = _sigmoid_tanh(_dot(acc.astype(BF16), wpg_ref[...]))
    emb = _rms(_dot(p_ref[...].astype(BF16), wple_ref[...]), gple_ref[...])
    out = acc + gate * emb
    if final:
        out = _rms(out, gfin_ref[...])
    o_ref[...] = out


def _mlp_call(x2d, p3d, layer, lw, g_final, tm, final):
    t = x2d.shape[0]
    row = lambda i: (i, 0)
    kernel = functools.partial(_mlp_kernel, final=final)
    in_specs = [
        pl.BlockSpec((tm, D_MODEL), row),
        pl.BlockSpec((None, tm, PLE_DIM), lambda i: (layer, i, 0)),
        _const_spec((1, D_MODEL)),
        _const_spec((D_MODEL, D_FF)),
        _const_spec((D_FF, D_MODEL)),
        _const_spec((D_MODEL, D_MODEL)),
        _const_spec((PLE_DIM, D_MODEL)),
        _const_spec((1, D_MODEL)),
        _const_spec((1, D_MODEL)),
    ]
    return pl.pallas_call(
        kernel, grid=(t // tm,), in_specs=in_specs,
        out_specs=pl.BlockSpec((tm, D_MODEL), row),
        out_shape=jax.ShapeDtypeStruct((t, D_MODEL), F32),
        compiler_params=pltpu.CompilerParams(dimension_semantics=("parallel",), vmem_limit_bytes=VMEM_LIMIT),
        name="mlp",
    )(x2d, p3d, lw["g_mlp"], lw["w_up"], lw["w_down"], lw["w_pg"], lw["w_ple"], lw["g_ple"], g_final)


def _rot_half_cols(w):
    half = QK_ROPE // 2
    return jnp.concatenate([-w[..., half:], w[..., :half]], axis=-1)


def _layer_weights(i, g_mix, w_in, g_q, w_uq, g_kv, w_ukv, w_oa, conv_w, conv_b, ln_g, ln_b, w_pc, b_gate,
                   w_out, g_mlp, w_up, w_down, w_ple_gate, w_ple, g_ple):
    wi = w_in[i]
    w_kr = wi[:, OFF_KR:OFF_CONV]
    w_kr2 = jnp.pad(jnp.concatenate([w_kr, _rot_half_cols(w_kr)], axis=1), ((0, 0), (0, LANES - 2 * QK_ROPE)))

    wq = w_uq[i].reshape(Q_LORA, N_HEADS, QK_NOPE + QK_ROPE)
    pad = jnp.zeros((Q_LORA, N_HEADS, LANES - QK_NOPE - QK_ROPE), F32)
    wq_plain = jnp.concatenate([wq, pad], axis=-1).reshape(Q_LORA, N_HEADS * LANES)
    wq_rot = jnp.concatenate([jnp.zeros((Q_LORA, N_HEADS, QK_NOPE), F32), _rot_half_cols(wq[..., QK_NOPE:]), pad],
                             axis=-1).reshape(Q_LORA, N_HEADS * LANES)

    wkv = w_ukv[i].reshape(KV_LORA, N_HEADS, QK_NOPE + V_HEAD)
    w_uk = wkv[..., :QK_NOPE].reshape(KV_LORA, N_HEADS * QK_NOPE)
    w_uv = wkv[..., QK_NOPE:].reshape(KV_LORA, N_HEADS * V_HEAD)
    r1 = lambda a: a[i].reshape(1, -1)
    return {
        "g_mix": r1(g_mix), "g_q": r1(g_q), "g_kv": r1(g_kv), "g_mlp": r1(g_mlp), "g_ple": r1(g_ple),
        "b_gate": r1(b_gate), "conv_b": r1(conv_b), "ln_g": r1(ln_g), "ln_b": r1(ln_b),
        "conv_w": conv_w[i],
        "w_lat": jnp.concatenate([wi[:, :OFF_KR], w_kr2], axis=1).astype(BF16),
        "w_conv": wi[:, OFF_CONV:OFF_GATE].astype(BF16),
        "w_gate": wi[:, OFF_GATE:].astype(BF16),
        "w_uq": jnp.concatenate([wq_plain, wq_rot], axis=1).astype(BF16),
        "w_ukt": w_uk.T.astype(BF16),
        "w_uv": w_uv.astype(BF16),
        "w_oa": w_oa[i].astype(BF16),
        "w_pc": w_pc[i].astype(BF16),
        "w_out": w_out[i].astype(BF16),
        "w_up": w_up[i].astype(BF16),
        "w_down": w_down[i].astype(BF16),
        "w_pg": w_ple_gate[i].astype(BF16),
        "w_ple": w_ple[i].astype(BF16),
    }


def _rope_tables(seq):
    inv = 1.0 / (ROPE_BASE ** (jnp.arange(0, QK_ROPE, 2, dtype=F32) / QK_ROPE))
    ang = jnp.arange(seq, dtype=F32)[:, None] * inv[None, :]
    cos, sin = jnp.cos(ang), jnp.sin(ang)
    scale = ATTN_SCALE * math.log2(math.e)
    ones = jnp.ones((seq, QK_NOPE), F32)
    zpad = jnp.zeros((seq, LANES - QK_NOPE - QK_ROPE), F32)
    cq = jnp.concatenate([ones, cos, cos, zpad], axis=1) * scale
    sq = jnp.concatenate([jnp.zeros((seq, QK_NOPE), F32), sin, sin, zpad], axis=1) * scale
    ckt = jnp.concatenate([cos, cos], axis=1).T
    skt = jnp.concatenate([sin, sin], axis=1).T
    return {"cq": cq, "sq": sq, "ckt": ckt, "skt": skt}


def _pick_tile(n, target):
    tile = min(n, target)
    assert n % tile == 0, (n, tile)
    return tile


def _run_group(x, p, layers, g_final):
    batch, seq, _ = x.shape
    depth = p.shape[0]
    tm = _pick_tile(seq, ROW_TILE)
    tq = _pick_tile(seq, Q_TILE)
    tk = _pick_tile(seq // 2, KV_CHUNK)
    tabs = _rope_tables(seq)
    x2d = x.reshape(batch * seq, D_MODEL)
    p3d = p.reshape(depth, batch * seq, PLE_DIM)
    gfin = g_final.reshape(1, D_MODEL)
    for i in range(depth):
        lw = layers[i]
        q, kt, v, glu = _pre_call(x2d, lw, tabs, batch, seq, _pick_tile(seq, PRE_TILE))
        attn = _attn_call(q, kt, v, batch, seq, tq, tk)
        x2d = _merge_call(x2d, attn, glu, lw, batch, seq, tm)
        x2d = _mlp_call(x2d, p3d, i, lw, gfin, tm, final=(i == depth - 1))
    return x2d.reshape(batch, seq, D_MODEL)


def kernel(x_prompt, x_sample, p_prompt, p_sample, g_mix, w_in, g_q, w_uq, g_kv, w_ukv, w_oa, conv_w, conv_b,
           ln_g, ln_b, w_pc, b_gate, w_out, g_mlp, w_up, w_down, w_ple_gate, w_ple, g_ple, g_final):
    depth = w_in.shape[0]
    layers = [_layer_weights(i, g_mix, w_in, g_q, w_uq, g_kv, w_ukv, w_oa, conv_w, conv_b, ln_g, ln_b, w_pc,
                             b_gate, w_out, g_mlp, w_up, w_down, w_ple_gate, w_ple, g_ple) for i in range(depth)]
    y_prompt = _run_group(x_prompt, p_prompt, layers, g_final)
    y_sample = _run_group(x_sample, p_sample, layers, g_final)
    return (y_prompt, y_sample)
```

```python
import functools
import math

import jax
import jax.numpy as jnp
from jax import lax
from jax.experimental import pallas as pl
from jax.experimental.pallas import tpu as pltpu

D_MODEL = 1024
N_HEADS = 8
Q_LORA = 256
KV_LORA = 128
QK_NOPE = 64
QK_ROPE = 32
V_HEAD = 64
ROPE_BASE = 10000.0
ATTN_SCALE = (QK_NOPE + QK_ROPE) ** -0.5
CONV_CH = 512
CONV_W = 31
CONV_PAD = (CONV_W - 1) // 2
D_FF = 4 * D_MODEL
PLE_DIM = 256
EPS = 1e-6

OFF_KV = Q_LORA
OFF_KR = OFF_KV + KV_LORA
OFF_CONV = OFF_KR + QK_ROPE
OFF_GATE = OFF_CONV + 2 * CONV_CH

LANES = 128
SUBLANES = 8
HALO = 16
VMEM_LIMIT = 56 * 1024 * 1024
FF_CHUNK = 1024
ROW_TILE = 512
PRE_TILE = 1024
Q_TILE = 512
KV_CHUNK = 2048

F32 = jnp.float32
BF16 = jnp.bfloat16


def _rms(x, g):
    return x * lax.rsqrt(jnp.mean(x * x, axis=-1, keepdims=True) + EPS) * g


def _sigmoid_tanh(x):
    return 0.5 * jnp.tanh(0.5 * x) + 0.5


def _dot(a, b):
    return jnp.dot(a, b, preferred_element_type=F32)


def _dot_nt(a, b):
    return lax.dot_general(a, b, (((1,), (1,)), ((), ())), preferred_element_type=F32)


def _const_spec(shape):
    return pl.BlockSpec(shape, lambda *_: (0,) * len(shape), pipeline_mode=pl.Buffered(1))


def _pre_kernel(x_ref, gmix_ref, wlat_ref, wconv_ref, gq_ref, wuq_ref, gkv_ref, wukt_ref,
                wuv_ref, cq_ref, sq_ref, ckt_ref, skt_ref, q_ref, kt_ref, v_ref, glu_ref):
    x = x_ref[...]
    u = _rms(x, gmix_ref[...]).astype(BF16)
    lat = _dot(u, wlat_ref[...])
    cqn = _rms(lat[:, :Q_LORA], gq_ref[...]).astype(BF16)
    ckvn = _rms(lat[:, Q_LORA:Q_LORA + KV_LORA], gkv_ref[...]).astype(BF16)
    kr2 = lat[:, Q_LORA + KV_LORA:].T

    q2 = _dot(cqn, wuq_ref[...])
    cq_t, sq_t = cq_ref[...], sq_ref[...]
    hw = N_HEADS * LANES
    for h in range(N_HEADS):
        lo = h * LANES
        q_ref[:, lo:lo + LANES] = (q2[:, lo:lo + LANES] * cq_t
                                   + q2[:, hw + lo:hw + lo + LANES] * sq_t).astype(BF16)

    v = _dot(ckvn, wuv_ref[...]).astype(BF16)
    ones = jnp.ones((v.shape[0], LANES), BF16)
    for g in range(N_HEADS // 2):
        v_ref[:, 2 * g * LANES:(2 * g + 1) * LANES] = v[:, g * LANES:(g + 1) * LANES]
        v_ref[:, (2 * g + 1) * LANES:(2 * g + 2) * LANES] = ones

    knt = _dot_nt(wukt_ref[...], ckvn)
    krt = (kr2[:QK_ROPE] * ckt_ref[...] + kr2[QK_ROPE:2 * QK_ROPE] * skt_ref[...]).astype(BF16)
    zeros = jnp.zeros((LANES - QK_NOPE - QK_ROPE, krt.shape[1]), BF16)
    for h in range(N_HEADS):
        kt_ref[0, h, 0:QK_NOPE, :] = knt[h * QK_NOPE:(h + 1) * QK_NOPE].astype(BF16)
        kt_ref[0, h, QK_NOPE:QK_NOPE + QK_ROPE, :] = krt
        kt_ref[0, h, QK_NOPE + QK_ROPE:, :] = zeros

    zc = _dot(u, wconv_ref[...])
    glu_ref[...] = zc[:, :CONV_CH] * _sigmoid_tanh(zc[:, CONV_CH:])


def _pre_call(x2d, lw, tabs, batch, seq, tm):
    t = batch * seq
    tiles_per_seq = seq // tm
    row = lambda i: (i, 0)
    in_specs = [
        pl.BlockSpec((tm, D_MODEL), row),
        _const_spec((1, D_MODEL)),
        _const_spec((D_MODEL, Q_LORA + KV_LORA + LANES)),
        _const_spec((D_MODEL, 2 * CONV_CH)),
        _const_spec((1, Q_LORA)),
        _const_spec((Q_LORA, 2 * N_HEADS * LANES)),
        _const_spec((1, KV_LORA)),
        _const_spec((N_HEADS * QK_NOPE, KV_LORA)),
        _const_spec((KV_LORA, N_HEADS * V_HEAD)),
        pl.BlockSpec((tm, LANES), lambda i: (i % tiles_per_seq, 0)),
        pl.BlockSpec((tm, LANES), lambda i: (i % tiles_per_seq, 0)),
        pl.BlockSpec((QK_ROPE, tm), lambda i: (0, i % tiles_per_seq)),
        pl.BlockSpec((QK_ROPE, tm), lambda i: (0, i % tiles_per_seq)),
    ]
    out_specs = [
        pl.BlockSpec((tm, N_HEADS * LANES), row),
        pl.BlockSpec((1, N_HEADS, LANES, tm), lambda i: (i // tiles_per_seq, 0, 0, i % tiles_per_seq)),
        pl.BlockSpec((tm, N_HEADS * LANES), row),
        pl.BlockSpec((tm, CONV_CH), row),
    ]
    out_shape = [
        jax.ShapeDtypeStruct((t, N_HEADS * LANES), BF16),
        jax.ShapeDtypeStruct((batch, N_HEADS, LANES, seq), BF16),
        jax.ShapeDtypeStruct((t, N_HEADS * LANES), BF16),
        jax.ShapeDtypeStruct((t, CONV_CH), F32),
    ]
    return pl.pallas_call(
        _pre_kernel, grid=(t // tm,), in_specs=in_specs, out_specs=out_specs, out_shape=out_shape,
        compiler_params=pltpu.CompilerParams(dimension_semantics=("parallel",), vmem_limit_bytes=VMEM_LIMIT),
        name="pre",
    )(x2d, lw["g_mix"], lw["w_lat"], lw["w_conv"], lw["g_q"], lw["w_uq"], lw["g_kv"],
      lw["w_ukt"], lw["w_uv"], tabs["cq"], tabs["sq"], tabs["ckt"], tabs["skt"])


def _attn_kernel(q_ref, kt_ref, v_ref, o_ref, s_sc, m_sc, ml_sc, acc_sc, *, tk, n_chunks):
    tq = q_ref.shape[0]
    reps = tk // LANES

    def chunk_start(c):
        return c * tk

    outs = []
    for hh in range(2):
        q = q_ref[:, hh * LANES:(hh + 1) * LANES]

        def stage_a(c, slot, q=q, hh=hh):
            s = _dot(q, kt_ref[0, hh, :, pl.ds(chunk_start(c), tk)])
            s_sc[slot] = s
            ml_sc[slot] = jnp.broadcast_to(jnp.max(s, axis=1, keepdims=True), (tq, LANES))

        def stage_b(c, slot):
            m_prev = m_sc[...]
            m_next = jnp.maximum(m_prev, ml_sc[slot])
            p = jnp.exp2(s_sc[slot] - jnp.tile(m_next, (1, reps)))
            alpha = jnp.exp2(m_prev - m_next)
            pv = _dot(p.astype(BF16), v_ref[pl.ds(chunk_start(c), tk), :])
            acc_sc[...] = jnp.tile(alpha, (1, 2)) * acc_sc[...] + pv
            m_sc[...] = m_next

        m_sc[...] = jnp.full(m_sc.shape, -jnp.inf, F32)
        acc_sc[...] = jnp.zeros(acc_sc.shape, F32)
        stage_a(0, 0)
        for c in range(n_chunks):
            if c + 1 < n_chunks:
                stage_a(c + 1, (c + 1) % 2)
            stage_b(c, c % 2)
        acc = acc_sc[...]
        outs.append(acc[:, :LANES] / acc[:, LANES:])
    lane = lax.broadcasted_iota(jnp.int32, (tq, LANES), 1)
    o_ref[...] = jnp.where(lane < V_HEAD, outs[0], outs[1]).astype(o_ref.dtype)


def _attn_call(q, kt, v, batch, seq, tq, tk):
    t = batch * seq
    nq = seq // tq
    pairs = N_HEADS // 2
    n_chunks = seq // tk
    kernel = functools.partial(_attn_kernel, tk=tk, n_chunks=n_chunks)
    return pl.pallas_call(
        kernel,
        grid=(batch, pairs, nq),
        in_specs=[
            pl.BlockSpec((tq, 2 * LANES), lambda b, g, i: (b * nq + i, g)),
            pl.BlockSpec((1, 2, LANES, seq), lambda b, g, i: (b, g, 0, 0)),
            pl.BlockSpec((seq, 2 * LANES), lambda b, g, i: (b, g)),
        ],
        out_specs=pl.BlockSpec((tq, 2 * V_HEAD), lambda b, g, i: (b * nq + i, g)),
        out_shape=jax.ShapeDtypeStruct((t, N_HEADS * V_HEAD), BF16),
        scratch_shapes=[pltpu.VMEM((2, tq, tk), F32), pltpu.VMEM((tq, LANES), F32),
                        pltpu.VMEM((2, tq, LANES), F32), pltpu.VMEM((tq, 2 * LANES), F32)],
        compiler_params=pltpu.CompilerParams(dimension_semantics=("parallel", "parallel", "arbitrary"),
                                             vmem_limit_bytes=VMEM_LIMIT),
        name="attn",
    )(q, kt, v)


def _merge_mlp_kernel(x_ref, attn_ref, gprev_ref, gcur_ref, gnext_ref, p_ref, gmix_ref, wgate_ref, bgate_ref,
                      woa_ref, convw_ref, convb_ref, lng_ref, lnb_ref, wpc_ref, wout_ref, gmlp_ref, wup_ref,
                      wdown_ref, wpg_ref, wple_ref, gple_ref, gfin_ref, o_ref, gext_sc, *, tiles_per_seq, final):
    tm = x_ref.shape[0]
    i = pl.program_id(0)
    x = x_ref[...]
    u = _rms(x, gmix_ref[...]).astype(BF16)
    gates = jax.nn.sigmoid(_dot(u, wgate_ref[...]) + bgate_ref[...])
    attn = _dot(attn_ref[...], woa_ref[...])

    pos = i % tiles_per_seq
    gext_sc[0:HALO, :] = jnp.where(pos == 0, 0.0, gprev_ref[...])
    gext_sc[HALO:HALO + tm, :] = gcur_ref[...]
    gext_sc[HALO + tm:, :] = jnp.where(pos == tiles_per_seq - 1, 0.0, gnext_ref[...])
    y = jnp.broadcast_to(convb_ref[...], (tm, CONV_CH))
    for r in range(SUBLANES):
        part = None
        for k in range(CONV_W):
            off = HALO - CONV_PAD + k
            if off % SUBLANES != r:
                continue
            base = off - r
            term = convw_ref[k:k + 1, :] * gext_sc[base:base + tm + SUBLANES, :]
            part = term if part is None else part + term
        if part is not None:
            y = y + part[r:r + tm, :]
    mu = jnp.mean(y, axis=-1, keepdims=True)
    yc = y - mu
    var = jnp.mean(yc * yc, axis=-1, keepdims=True)
    yn = yc * lax.rsqrt(var + EPS) * lng_ref[...] + lnb_ref[...]
    conv = _dot((yn * jax.nn.sigmoid(yn)).astype(BF16), wpc_ref[...])

    merged = gates[:, :D_MODEL] * attn + gates[:, D_MODEL:] * conv
    x = x + _dot(merged.astype(BF16), wout_ref[...])

    un = _rms(x, gmlp_ref[...]).astype(BF16)
    acc = x
    for c in range(D_FF // FF_CHUNK):
        lo = c * FF_CHUNK
        h = jnp.maximum(_dot(un, wup_ref[:, lo:lo + FF_CHUNK]), 0.0)
        acc = acc + _dot((h * h).astype(BF16), wdown_ref[lo:lo + FF_CHUNK, :])
    gate = _sigmoid_tanh(_dot(acc.astype(BF16), wpg_ref[...]))
    emb = _rms(_dot(p_ref[...].astype(BF16), wple_ref[...]), gple_ref[...])
    out = acc + gate * emb
    if final:
        out = _rms(out, gfin_ref[...])
    o_ref[...] = out


def _merge_mlp_call(x2d, attn, glu, p3d, layer, lw, g_final, batch, seq, tm, final):
    t = batch * seq
    tiles_per_seq = seq // tm
    hb = tm // HALO
    n_hblk = t // HALO
    row = lambda i: (i, 0)
    kernel = functools.partial(_merge_mlp_kernel, tiles_per_seq=tiles_per_seq, final=final)
    in_specs = [
        pl.BlockSpec((tm, D_MODEL), row),
        pl.BlockSpec((tm, N_HEADS * V_HEAD), row),
        pl.BlockSpec((HALO, CONV_CH), lambda i: (jnp.maximum(i * hb - 1, 0), 0)),
        pl.BlockSpec((tm, CONV_CH), row),
        pl.BlockSpec((HALO, CONV_CH), lambda i: (jnp.minimum((i + 1) * hb, n_hblk - 1), 0)),
        pl.BlockSpec((None, tm, PLE_DIM), lambda i: (layer, i, 0)),
        _const_spec((1, D_MODEL)),
        _const_spec((D_MODEL, 2 * D_MODEL)),
        _const_spec((1, 2 * D_MODEL)),
        _const_spec((N_HEADS * V_HEAD, D_MODEL)),
        _const_spec((CONV_W, CONV_CH)),
        _const_spec((1, CONV_CH)),
        _const_spec((1, CONV_CH)),
        _const_spec((1, CONV_CH)),
        _const_spec((CONV_CH, D_MODEL)),
        _const_spec((D_MODEL, D_MODEL)),
        _const_spec((1, D_MODEL)),
        _const_spec((D_MODEL, D_FF)),
        _const_spec((D_FF, D_MODEL)),
        _const_spec((D_MODEL, D_MODEL)),
        _const_spec((PLE_DIM, D_MODEL)),
        _const_spec((1, D_MODEL)),
        _const_spec((1, D_MODEL)),
    ]
    return pl.pallas_call(
        kernel, grid=(t // tm,), in_specs=in_specs,
        out_specs=pl.BlockSpec((tm, D_MODEL), row),
        out_shape=jax.ShapeDtypeStruct((t, D_MODEL), F32),
        scratch_shapes=[pltpu.VMEM((tm + 2 * HALO, CONV_CH), F32)],
        compiler_params=pltpu.CompilerParams(dimension_semantics=("parallel",), vmem_limit_bytes=VMEM_LIMIT),
        name="mergemlp",
    )(x2d, attn, glu, glu, glu, p3d, lw["g_mix"], lw["w_gate"], lw["b_gate"], lw["w_oa"], lw["conv_w"],
      lw["conv_b"], lw["ln_g"], lw["ln_b"], lw["w_pc"], lw["w_out"], lw["g_mlp"], lw["w_up"], lw["w_down"],
      lw["w_pg"], lw["w_ple"], lw["g_ple"], g_final)


def _rot_half_cols(w):
    half = QK_ROPE // 2
    return jnp.concatenate([-w[..., half:], w[..., :half]], axis=-1)


def _layer_weights(i, g_mix, w_in, g_q, w_uq, g_kv, w_ukv, w_oa, conv_w, conv_b, ln_g, ln_b, w_pc, b_gate,
                   w_out, g_mlp, w_up, w_down, w_ple_gate, w_ple, g_ple):
    wi = w_in[i]
    w_kr = wi[:, OFF_KR:OFF_CONV]
    w_kr2 = jnp.pad(jnp.concatenate([w_kr, _rot_half_cols(w_kr)], axis=1), ((0, 0), (0, LANES - 2 * QK_ROPE)))

    wq = w_uq[i].reshape(Q_LORA, N_HEADS, QK_NOPE + QK_ROPE)
    pad = jnp.zeros((Q_LORA, N_HEADS, LANES - QK_NOPE - QK_ROPE), F32)
    wq_plain = jnp.concatenate([wq, pad], axis=-1).reshape(Q_LORA, N_HEADS * LANES)
    wq_rot = jnp.concatenate([jnp.zeros((Q_LORA, N_HEADS, QK_NOPE), F32), _rot_half_cols(wq[..., QK_NOPE:]), pad],
                             axis=-1).reshape(Q_LORA, N_HEADS * LANES)

    wkv = w_ukv[i].reshape(KV_LORA, N_HEADS, QK_NOPE + V_HEAD)
    w_uk = wkv[..., :QK_NOPE].reshape(KV_LORA, N_HEADS * QK_NOPE)
    w_uv = wkv[..., QK_NOPE:].reshape(KV_LORA, N_HEADS * V_HEAD)
    r1 = lambda a: a[i].reshape(1, -1)
    return {
        "g_mix": r1(g_mix), "g_q": r1(g_q), "g_kv": r1(g_kv), "g_mlp": r1(g_mlp), "g_ple": r1(g_ple),
        "b_gate": r1(b_gate), "conv_b": r1(conv_b), "ln_g": r1(ln_g), "ln_b": r1(ln_b),
        "conv_w": conv_w[i],
        "w_lat": jnp.concatenate([wi[:, :OFF_KR], w_kr2], axis=1).astype(BF16),
        "w_conv": wi[:, OFF_CONV:OFF_GATE].astype(BF16),
        "w_gate": wi[:, OFF_GATE:].astype(BF16),
        "w_uq": jnp.concatenate([wq_plain, wq_rot], axis=1).astype(BF16),
        "w_ukt": w_uk.T.astype(BF16),
        "w_uv": w_uv.astype(BF16),
        "w_oa": w_oa[i].astype(BF16),
        "w_pc": w_pc[i].astype(BF16),
        "w_out": w_out[i].astype(BF16),
        "w_up": w_up[i].astype(BF16),
        "w_down": w_down[i].astype(BF16),
        "w_pg": w_ple_gate[i].astype(BF16),
        "w_ple": w_ple[i].astype(BF16),
    }


def _rope_tables(seq):
    inv = 1.0 / (ROPE_BASE ** (jnp.arange(0, QK_ROPE, 2, dtype=F32) / QK_ROPE))
    ang = jnp.arange(seq, dtype=F32)[:, None] * inv[None, :]
    cos, sin = jnp.cos(ang), jnp.sin(ang)
    scale = ATTN_SCALE * math.log2(math.e)
    ones = jnp.ones((seq, QK_NOPE), F32)
    zpad = jnp.zeros((seq, LANES - QK_NOPE - QK_ROPE), F32)
    cq = jnp.concatenate([ones, cos, cos, zpad], axis=1) * scale
    sq = jnp.concatenate([jnp.zeros((seq, QK_NOPE), F32), sin, sin, zpad], axis=1) * scale
    ckt = jnp.concatenate([cos, cos], axis=1).T
    skt = jnp.concatenate([sin, sin], axis=1).T
    return {"cq": cq, "sq": sq, "ckt": ckt, "skt": skt}


def _pick_tile(n, target):
    tile = min(n, target)
    assert n % tile == 0, (n, tile)
    return tile


def _run_group(x, p, layers, g_final):
    batch, seq, _ = x.shape
    depth = p.shape[0]
    tm = _pick_tile(seq, ROW_TILE)
    tq = _pick_tile(seq, Q_TILE)
    tk = _pick_tile(seq // 2, KV_CHUNK)
    tabs = _rope_tables(seq)
    x2d = x.reshape(batch * seq, D_MODEL)
    p3d = p.reshape(depth, batch * seq, PLE_DIM)
    gfin = g_final.reshape(1, D_MODEL)
    for i in range(depth):
        lw = layers[i]
        q, kt, v, glu = _pre_call(x2d, lw, tabs, batch, seq, _pick_tile(seq, PRE_TILE))
        attn = _attn_call(q, kt, v, batch, seq, tq, tk)
        x2d = _merge_mlp_call(x2d, attn, glu, p3d, i, lw, gfin, batch, seq, tm, final=(i == depth - 1))
    return x2d.reshape(batch, seq, D_MODEL)


def kernel(x_prompt, x_sample, p_prompt, p_sample, g_mix, w_in, g_q, w_uq, g_kv, w_ukv, w_oa, conv_w, conv_b,
           ln_g, ln_b, w_pc, b_gate, w_out, g_mlp, w_up, w_down, w_ple_gate, w_ple, g_ple, g_final):
    depth = w_in.shape[0]
    layers = [_layer_weights(i, g_mix, w_in, g_q, w_uq, g_kv, w_ukv, w_oa, conv_w, conv_b, ln_g, ln_b, w_pc,
                             b_gate, w_out, g_mlp, w_up, w_down, w_ple_gate, w_ple, g_ple) for i in range(depth)]
    y_prompt = _run_group(x_prompt, p_prompt, layers, g_final)
    y_sample = _run_group(x_sample, p_sample, layers, g_final)
    return (y_prompt, y_sample)
```
